```python
import math
import jax, jax.numpy as jnp
from jax import lax
import numpy as np

D_MODEL = 1024
BATCH = 8
SEQ = 4096
DEPTH = 1

PLE_DIM = 256
N_HEADS = 4
QK_NOPE_DIM = 128
QK_ROPE_DIM = 64
V_HEAD_DIM = 128
Q_LORA_RANK = 256
KV_LORA_RANK = 128
ROPE_THETA = 10000.0
ATTN_WIDTH = N_HEADS * V_HEAD_DIM
Q_BLOCK = 128
CONV_GROUPS = 4
CONV_WIDTH = 512
CONV_K = 3
MIX_WIDTH = ATTN_WIDTH + CONV_WIDTH
IN_PROJ_WIDTH = Q_LORA_RANK + KV_LORA_RANK + QK_ROPE_DIM + 3 * CONV_WIDTH
N_EXPERTS = 32
TOP_K = 4
D_EXPERT = 1024
SWIGLU_ALPHA = 1.702
SWIGLU_LIMIT = 7.0
EXPERT_BLOCK = 128
EPS = 1e-6

kernel_name = "hymba_mla_shortconv_moe_ple"


def rmsnorm(x, g):
    xf = x.astype(jnp.float32)
    y = xf * lax.rsqrt(jnp.mean(xf * xf, axis=-1, keepdims=True) + EPS)
    return (y * g.astype(jnp.float32)).astype(x.dtype)


def apply_rope(x, positions):
    half = QK_ROPE_DIM // 2
    inv_freq = ROPE_THETA ** (-jnp.arange(half, dtype=jnp.float32) / half)
    ang = positions.astype(jnp.float32)[..., None] * inv_freq
    cos = jnp.cos(ang)[:, :, None, :]
    sin = jnp.sin(ang)[:, :, None, :]
    xf = x.astype(jnp.float32)
    x1, x2 = xf[..., :half], xf[..., half:]
    return jnp.concatenate([x1 * cos - x2 * sin, x1 * sin + x2 * cos], axis=-1).astype(x.dtype)


def mla(c_q, c_kv, k_pe, positions, q_norm, w_uq, kv_norm, w_ukv):
    B, S, _ = c_q.shape
    q = (rmsnorm(c_q, q_norm) @ w_uq).reshape(B, S, N_HEADS, QK_NOPE_DIM + QK_ROPE_DIM)
    q_nope, q_pe = q[..., :QK_NOPE_DIM], q[..., QK_NOPE_DIM:]
    q_pe = apply_rope(q_pe, positions)
    kv = (rmsnorm(c_kv, kv_norm) @ w_ukv).reshape(B, S, N_HEADS, QK_NOPE_DIM + V_HEAD_DIM)
    k_nope, v = kv[..., :QK_NOPE_DIM], kv[..., QK_NOPE_DIM:]
    k_pe = apply_rope(k_pe[:, :, None, :], positions)[:, :, 0, :]
    scale = (QK_NOPE_DIM + QK_ROPE_DIM) ** -0.5
    nqb = S // Q_BLOCK
    q_nope_b = q_nope.reshape(B, nqb, Q_BLOCK, N_HEADS, QK_NOPE_DIM).transpose(1, 0, 2, 3, 4)
    q_pe_b = q_pe.reshape(B, nqb, Q_BLOCK, N_HEADS, QK_ROPE_DIM).transpose(1, 0, 2, 3, 4)
    k_pos = jnp.arange(S)

    def q_block(args):
        qn, qp, i = args
        s = (jnp.einsum('bqhd,bkhd->bhqk', qn, k_nope)
             + jnp.einsum('bqhr,bkr->bhqk', qp, k_pe)).astype(jnp.float32) * scale
        q_pos = i * Q_BLOCK + jnp.arange(Q_BLOCK)
        causal = k_pos[None, :] <= q_pos[:, None]
        s = jnp.where(causal, s, -jnp.inf)
        prob = jax.nn.softmax(s, axis=-1).astype(v.dtype)
        return jnp.einsum('bhqk,bkhd->bqhd', prob, v)

    out = lax.map(q_block, (q_nope_b, q_pe_b, jnp.arange(nqb)))
    return out.transpose(1, 0, 2, 3, 4).reshape(B, S, ATTN_WIDTH)


def short_conv(b_gate, c_gate, val, conv_w):
    S = val.shape[1]
    u = c_gate * val
    u_pad = jnp.pad(u, ((0, 0), (CONV_K - 1, 0), (0, 0)))
    y = conv_w[0] * u_pad[:, 0:S]
    for k in range(1, CONV_K):
        y = y + conv_w[k] * u_pad[:, k:k + S]
    return b_gate * y


def clamped_swiglu(h):
    gate, up = h[..., :D_EXPERT], h[..., D_EXPERT:]
    gate = jnp.minimum(gate, SWIGLU_LIMIT)
    up = jnp.clip(up, -SWIGLU_LIMIT, SWIGLU_LIMIT)
    return gate * jax.nn.sigmoid(SWIGLU_ALPHA * gate) * (up + 1)


def moe(x, w_router, b_router, w1, b1, w2, b2):
    B, S, D = x.shape
    N = B * S
    xt = x.reshape(N, D)
    logits = (xt @ w_router + b_router).astype(jnp.float32)
    top_v, top_i = lax.top_k(logits, TOP_K)
    gates = jax.nn.softmax(top_v, axis=-1)
    M = N * TOP_K
    e = top_i.reshape(M)
    tok = jnp.repeat(jnp.arange(N, dtype=jnp.int32), TOP_K)
    g = gates.reshape(M)
    order = jnp.argsort(e)
    e_s, tok_s, g_s = e[order], tok[order], g[order]
    counts = jnp.bincount(e, length=N_EXPERTS)
    starts = jnp.cumsum(counts) - counts
    padded = ((counts + EXPERT_BLOCK - 1) // EXPERT_BLOCK) * EXPERT_BLOCK
    pad_ends = jnp.cumsum(padded)
    pad_starts = pad_ends - padded
    dest = pad_starts[e_s] + jnp.arange(M) - starts[e_s]
    n_blocks = -(-M // EXPERT_BLOCK) + N_EXPERTS
    P = n_blocks * EXPERT_BLOCK
    row_tok = jnp.full((P,), N, dtype=jnp.int32).at[dest].set(tok_s)
    row_gate = jnp.zeros((P,), dtype=jnp.float32).at[dest].set(g_s)
    blk_expert = jnp.clip(jnp.searchsorted(pad_ends, jnp.arange(n_blocks) * EXPERT_BLOCK, side='right'),
                          0, N_EXPERTS - 1)
    x_pad = jnp.concatenate([xt, jnp.zeros((1, D), xt.dtype)], axis=0)

    def expert_block(args):
        rows, eid = args
        xb = x_pad[rows]
        hb = clamped_swiglu(xb @ w1[eid] + b1[eid])
        return hb @ w2[eid] + b2[eid]

    y = lax.map(expert_block, (row_tok.reshape(n_blocks, EXPERT_BLOCK), blk_expert))
    y = y.reshape(P, D) * row_gate[:, None].astype(y.dtype)
    out = jax.ops.segment_sum(y, row_tok, num_segments=N + 1)[:N]
    return out.reshape(B, S, D)


def setup_inputs(seed: int = 0) -> dict:
    key = jax.random.key(seed)
    ks = jax.random.split(key, 32)
    f32 = jnp.float32

    def nrm(k, shape, fan_in):
        return jax.random.normal(k, shape, f32) * (fan_in ** -0.5)

    def gain(k, shape):
        return 1.0 + 0.01 * jax.random.normal(k, shape, f32)

    L = DEPTH
    return {
        "x": jax.random.normal(ks[0], (BATCH, SEQ, D_MODEL), f32),
        "p": jax.random.normal(ks[1], (DEPTH, BATCH, SEQ, PLE_DIM), f32),
        "positions": jnp.broadcast_to(jnp.arange(SEQ, dtype=jnp.int32), (BATCH, SEQ)),
        "mix_norm": gain(ks[2], (L, D_MODEL)),
        "w_in": nrm(ks[3], (L, D_MODEL, IN_PROJ_WIDTH), D_MODEL),
        "q_norm": gain(ks[4], (L, Q_LORA_RANK)),
        "w_uq": nrm(ks[5], (L, Q_LORA_RANK, N_HEADS * (QK_NOPE_DIM + QK_ROPE_DIM)), Q_LORA_RANK),
        "kv_norm": gain(ks[6], (L, KV_LORA_RANK)),
        "w_ukv": nrm(ks[7], (L, KV_LORA_RANK, N_HEADS * (QK_NOPE_DIM + V_HEAD_DIM)), KV_LORA_RANK),
        "conv_w": nrm(ks[8], (L, CONV_K, CONV_WIDTH), CONV_K),
        "attn_out_norm": gain(ks[9], (L, ATTN_WIDTH)),
        "conv_out_norm": gain(ks[10], (L, CONV_WIDTH)),
        "w_o": nrm(ks[11], (L, MIX_WIDTH, D_MODEL), MIX_WIDTH),
        "moe_norm": gain(ks[12], (L, D_MODEL)),
        "w_router": nrm(ks[13], (L, D_MODEL, N_EXPERTS), D_MODEL),
        "b_router": 0.01 * jax.random.normal(ks[14], (L, N_EXPERTS), f32),
        "w1": nrm(ks[15], (L, N_EXPERTS, D_MODEL, 2 * D_EXPERT), D_MODEL),
        "b1": 0.01 * jax.random.normal(ks[16], (L, N_EXPERTS, 2 * D_EXPERT), f32),
        "w2": nrm(ks[17], (L, N_EXPERTS, D_EXPERT, D_MODEL), D_EXPERT),
        "b2": 0.01 * jax.random.normal(ks[18], (L, N_EXPERTS, D_MODEL), f32),
        "ple_norm": gain(ks[19], (L, D_MODEL)),
        "w_ple_gate": nrm(ks[20], (L, D_MODEL, D_MODEL), D_MODEL),
        "w_ple_proj": nrm(ks[21], (L, PLE_DIM, D_MODEL), PLE_DIM),
        "ple_post_norm": gain(ks[22], (L, D_MODEL)),
        "final_norm": gain(ks[23], (D_MODEL,)),
    }


def reference(x, p, positions, mix_norm, w_in, q_norm, w_uq, kv_norm, w_ukv, conv_w,
              attn_out_norm, conv_out_norm, w_o, moe_norm, w_router, b_router, w1, b1, w2, b2,
              ple_norm, w_ple_gate, w_ple_proj, ple_post_norm, final_norm):
    split_pts = list(np.cumsum([Q_LORA_RANK, KV_LORA_RANK, QK_ROPE_DIM, CONV_WIDTH, CONV_WIDTH]))
    h = x
    for i in range(DEPTH):
        z = rmsnorm(h, mix_norm[i]) @ w_in[i]
        c_q, c_kv, k_pe, b_gate, c_gate, val = jnp.split(z, split_pts, axis=-1)
        attn = mla(c_q, c_kv, k_pe, positions, q_norm[i], w_uq[i], kv_norm[i], w_ukv[i])
        conv = short_conv(b_gate, c_gate, val, conv_w[i])
        mixed = jnp.concatenate([rmsnorm(attn, attn_out_norm[i]),
                                 rmsnorm(conv, conv_out_norm[i])], axis=-1)
        h = h + mixed @ w_o[i]
        h = h + moe(rmsnorm(h, moe_norm[i]), w_router[i], b_router[i], w1[i], b1[i], w2[i], b2[i])
        gate = jax.nn.sigmoid(rmsnorm(h, ple_norm[i]) @ w_ple_gate[i])
        ple = rmsnorm(p[i] @ w_ple_proj[i], ple_post_norm[i])
        h = h + gate * ple
    return rmsnorm(h, final_norm)
```

```python
import functools

import jax
import jax.numpy as jnp
from jax import lax
from jax.experimental import pallas as pl
from jax.experimental.pallas import tpu as pltpu

D_MODEL = 1024
PLE_DIM = 256
N_HEADS = 4
QK_NOPE_DIM = 128
QK_ROPE_DIM = 64
ROPE_HALF = QK_ROPE_DIM // 2
V_HEAD_DIM = 128
QK_DIM = QK_NOPE_DIM + QK_ROPE_DIM
Q_LORA_RANK = 256
KV_LORA_RANK = 128
ROPE_THETA = 10000.0
ATTN_WIDTH = N_HEADS * V_HEAD_DIM
CONV_WIDTH = 512
CONV_K = 3
N_EXPERTS = 32
TOP_K = 4
D_EXPERT = 1024
SWIGLU_ALPHA = 1.702
SWIGLU_LIMIT = 7.0
EPS = 1e-6

SUBLANES = 8
LANES = 128
ROW_CHUNKS = D_MODEL // LANES

TM_IN = 512
TQ = 512
TK = 512
TM_OUT = 512
TD = 256
TMF = 256
TF = 256

VMEM_LIMIT = 56 * 1024 * 1024

BF16 = jnp.bfloat16
F32 = jnp.float32


def _rms(x, g):
    return x * lax.rsqrt(jnp.mean(x * x, axis=-1, keepdims=True) + EPS) * g


def _inproj_kernel(x_ref, pos_ref, invf_ref, gmix_ref, win_ref, gq_ref, wuq_ref, gkv_ref, wukv_ref,
                   convw_ref, gconv_ref, q_ref, k_ref, v_ref, conv_ref, ubuf_ref):
    tm = x_ref.shape[1]
    s_idx = pl.program_id(1)

    xn = _rms(x_ref[0], gmix_ref[...])
    z = jnp.dot(xn.astype(BF16), win_ref[...], preferred_element_type=F32)
    c_q = z[:, 0:Q_LORA_RANK]
    c_kv = z[:, Q_LORA_RANK:Q_LORA_RANK + KV_LORA_RANK]
    o = Q_LORA_RANK + KV_LORA_RANK
    b_gate = z[:, o:o + CONV_WIDTH]
    c_gate = z[:, o + CONV_WIDTH:o + 2 * CONV_WIDTH]
    val = z[:, o + 2 * CONV_WIDTH:o + 3 * CONV_WIDTH]
    k_pe = z[:, o + 3 * CONV_WIDTH:o + 3 * CONV_WIDTH + QK_ROPE_DIM]

    ang = pos_ref[0].astype(F32) * invf_ref[...]
    cos4 = jnp.cos(ang)
    sin4 = jnp.sin(ang)

    q = jnp.dot(_rms(c_q, gq_ref[...]).astype(BF16), wuq_ref[...], preferred_element_type=F32)
    qn = N_HEADS * QK_NOPE_DIM
    x1 = q[:, qn:qn + LANES]
    x2 = q[:, qn + LANES:qn + 2 * LANES]
    o1 = x1 * cos4 - x2 * sin4
    o2 = x1 * sin4 + x2 * cos4
    scale = QK_DIM ** -0.5

    kv = jnp.dot(_rms(c_kv, gkv_ref[...]).astype(BF16), wukv_ref[...], preferred_element_type=F32)
    k1 = k_pe[:, :ROPE_HALF]
    k2 = k_pe[:, ROPE_HALF:]
    c1 = cos4[:, :ROPE_HALF]
    s1 = sin4[:, :ROPE_HALF]
    kr = jnp.concatenate([k1 * c1 - k2 * s1, k1 * s1 + k2 * c1], axis=1)

    for h in range(N_HEADS):
        qh = jnp.concatenate(
            [q[:, h * QK_NOPE_DIM:(h + 1) * QK_NOPE_DIM],
             o1[:, h * ROPE_HALF:(h + 1) * ROPE_HALF],
             o2[:, h * ROPE_HALF:(h + 1) * ROPE_HALF]], axis=1) * scale
        q_ref[0, h] = qh.astype(BF16)
        base = h * (QK_NOPE_DIM + V_HEAD_DIM)
        kh = jnp.concatenate([kv[:, base:base + QK_NOPE_DIM], kr], axis=1)
        k_ref[0, h] = kh.astype(BF16)
        v_ref[0, h] = kv[:, base + QK_NOPE_DIM:base + QK_NOPE_DIM + V_HEAD_DIM].astype(BF16)

    u = c_gate * val

    @pl.when(s_idx == 0)
    def _():
        ubuf_ref[0:SUBLANES, :] = jnp.zeros((SUBLANES, CONV_WIDTH), F32)

    ubuf_ref[SUBLANES:SUBLANES + tm, :] = u
    u_m1 = ubuf_ref[SUBLANES - 1:SUBLANES - 1 + tm, :]
    u_m2 = ubuf_ref[SUBLANES - 2:SUBLANES - 2 + tm, :]
    cw = convw_ref[...]
    y = cw[0:1, :] * u_m2 + cw[1:2, :] * u_m1 + cw[2:3, :] * u
    conv_ref[0] = _rms(b_gate * y, gconv_ref[...]).astype(BF16)
    ubuf_ref[0:SUBLANES, :] = ubuf_ref[tm:tm + SUBLANES, :]


def _inproj(x, positions, invf4, mix_norm, w_in_r, q_norm, w_uq_r, kv_norm, w_ukv, conv_w, conv_out_norm):
    B, S, D = x.shape
    tm = TM_IN
    const = lambda b, s: (0, 0)
    return pl.pallas_call(
        _inproj_kernel,
        grid=(B, S // tm),
        in_specs=[
            pl.BlockSpec((1, tm, D), lambda b, s: (b, s, 0)),
            pl.BlockSpec((1, tm, 1), lambda b, s: (b, s, 0)),
            pl.BlockSpec(invf4.shape, const),
            pl.BlockSpec(mix_norm.shape, const),
            pl.BlockSpec(w_in_r.shape, const),
            pl.BlockSpec(q_norm.shape, const),
            pl.BlockSpec(w_uq_r.shape, const),
            pl.BlockSpec(kv_norm.shape, const),
            pl.BlockSpec(w_ukv.shape, const),
            pl.BlockSpec(conv_w.shape, const),
            pl.BlockSpec(conv_out_norm.shape, const),
        ],
        out_specs=[
            pl.BlockSpec((1, N_HEADS, tm, QK_DIM), lambda b, s: (b, 0, s, 0)),
            pl.BlockSpec((1, N_HEADS, tm, QK_DIM), lambda b, s: (b, 0, s, 0)),
            pl.BlockSpec((1, N_HEADS, tm, V_HEAD_DIM), lambda b, s: (b, 0, s, 0)),
            pl.BlockSpec((1, tm, CONV_WIDTH), lambda b, s: (b, s, 0)),
        ],
        out_shape=[
            jax.ShapeDtypeStruct((B, N_HEADS, S, QK_DIM), BF16),
            jax.ShapeDtypeStruct((B, N_HEADS, S, QK_DIM), BF16),
            jax.ShapeDtypeStruct((B, N_HEADS, S, V_HEAD_DIM), BF16),
            jax.ShapeDtypeStruct((B, S, CONV_WIDTH), BF16),
        ],
        scratch_shapes=[pltpu.VMEM((tm + 2 * SUBLANES, CONV_WIDTH), F32)],
        compiler_params=pltpu.CompilerParams(
            dimension_semantics=("arbitrary", "arbitrary"), vmem_limit_bytes=VMEM_LIMIT),
        name="inproj",
    )(x, positions, invf4, mix_norm, w_in_r, q_norm, w_uq_r, kv_norm, w_ukv, conv_w, conv_out_norm)


def _attn_kernel(q_ref, k_ref, v_ref, o_ref):
    tq = q_ref.shape[2]
    i = pl.program_id(2)
    q = q_ref[0, 0]
    neg = jnp.float32(-1e30)

    def step(j, carry, masked):
        m, l, acc = carry
        start = pl.multiple_of(j * TK, TK)
        kj = k_ref[0, 0, pl.ds(start, TK), :]
        vj = v_ref[0, 0, pl.ds(start, TK), :]
        s = lax.dot_general(q, kj, (((1,), (1,)), ((), ())), preferred_element_type=F32)
        if masked:
            row = lax.broadcasted_iota(jnp.int32, (tq, TK), 0)
            col = lax.broadcasted_iota(jnp.int32, (tq, TK), 1)
            s = jnp.where(col <= row, s, neg)
        m_new = jnp.maximum(m, jnp.max(s, axis=-1, keepdims=True))
        alpha = jnp.exp(m - m_new)
        p = jnp.exp(s - m_new)
        l = alpha * l + jnp.sum(p, axis=-1, keepdims=True)
        acc = alpha * acc + jnp.dot(p.astype(BF16), vj, preferred_element_type=F32)
        return m_new, l, acc

    init = (jnp.full((tq, 1), neg, F32), jnp.zeros((tq, 1), F32), jnp.zeros((tq, V_HEAD_DIM), F32))
    carry = lax.fori_loop(0, i, lambda j, c: step(j, c, False), init)
    m, l, acc = step(i, carry, True)
    o_ref[0] = (acc / l).astype(BF16)


def _attention(q, k, v):
    B, H, S, _ = q.shape
    assert TQ == TK
    return pl.pallas_call(
        _attn_kernel,
        grid=(B, H, S // TQ),
        in_specs=[
            pl.BlockSpec((1, 1, TQ, QK_DIM), lambda b, h, i: (b, h, i, 0)),
            pl.BlockSpec((1, 1, S, QK_DIM), lambda b, h, i: (b, h, 0, 0)),
            pl.BlockSpec((1, 1, S, V_HEAD_DIM), lambda b, h, i: (b, h, 0, 0)),
        ],
        out_specs=pl.BlockSpec((1, TQ, V_HEAD_DIM), lambda b, h, i: (b, i, h)),
        out_shape=jax.ShapeDtypeStruct((B, S, H * V_HEAD_DIM), BF16),
        compiler_params=pltpu.CompilerParams(
            dimension_semantics=("arbitrary", "arbitrary", "arbitrary"), vmem_limit_bytes=VMEM_LIMIT),
        name="attention",
    )(q, k, v)


def _outproj_kernel(x_ref, attn_ref, conv_ref, ga_ref, wo_ref, gmoe_ref, wr_ref, br_ref,
                    h1_ref, hn_ref, route_ref, gate_ref, counts_ref, carry_ref):
    tm = x_ref.shape[0]
    step = pl.program_id(0)

    @pl.when(step == 0)
    def _():
        carry_ref[...] = jnp.zeros_like(carry_ref)

    attn_n = _rms(attn_ref[...].astype(F32), ga_ref[...]).astype(BF16)
    mixed = jnp.concatenate([attn_n, conv_ref[...]], axis=1)
    h1 = x_ref[...] + jnp.dot(mixed, wo_ref[...], preferred_element_type=F32)
    h1_ref[...] = h1

    hn = _rms(h1, gmoe_ref[...])
    for j in range(ROW_CHUNKS):
        hn_ref[pl.ds(j, tm, stride=SUBLANES), :] = hn[:, j * LANES:(j + 1) * LANES]

    logits = jnp.dot(hn, wr_ref[...], preferred_element_type=F32,
                     precision=lax.Precision.HIGHEST) + br_ref[...]

    lane = lax.broadcasted_iota(jnp.int32, (tm, N_EXPERTS), 1).astype(F32)
    work = logits
    vals, idxs, hots = [], [], []
    for _ in range(TOP_K):
        mx = jnp.max(work, axis=-1, keepdims=True)
        idx = jnp.min(jnp.where(work == mx, lane, float(N_EXPERTS)), axis=-1, keepdims=True)
        hot = lane == idx
        work = jnp.where(hot, -jnp.inf, work)
        vals.append(mx)
        idxs.append(idx)
        hots.append(hot)

    exps = [jnp.exp(v - vals[0]) for v in vals]
    denom = exps[0] + exps[1] + exps[2] + exps[3]
    gates = [e / denom for e in exps]

    sel = (hots[0] | hots[1] | hots[2] | hots[3]).astype(F32)
    r = lax.broadcasted_iota(jnp.int32, (tm, tm), 0)
    c = lax.broadcasted_iota(jnp.int32, (tm, tm), 1)
    tri = (c < r).astype(BF16)
    rank_mat = jnp.dot(tri, sel.astype(BF16), preferred_element_type=F32) + carry_ref[...]
    carry_ref[...] = carry_ref[...] + jnp.sum(sel, axis=0, keepdims=True)
    counts_ref[...] = carry_ref[...].astype(jnp.int32)

    ranks = [jnp.sum(jnp.where(h, rank_mat, 0.0), axis=-1, keepdims=True) for h in hots]
    lane8 = lax.broadcasted_iota(jnp.int32, (tm, 2 * TOP_K), 1)
    route = jnp.zeros((tm, 2 * TOP_K), F32)
    gate_out = jnp.zeros((tm, TOP_K), F32)
    lane4 = lax.broadcasted_iota(jnp.int32, (tm, TOP_K), 1)
    for k in range(TOP_K):
        route = jnp.where(lane8 == k, idxs[k], route)
        route = jnp.where(lane8 == TOP_K + k, ranks[k], route)
        gate_out = jnp.where(lane4 == k, gates[k], gate_out)
    route_ref[...] = route.astype(jnp.int32)
    gate_ref[...] = gate_out


def _outproj(x2, attn2, conv2, attn_out_norm, w_o, moe_norm, w_router, b_router):
    N, D = x2.shape
    tm = TM_OUT
    const = lambda i: (0, 0)
    return pl.pallas_call(
        _outproj_kernel,
        grid=(N // tm,),
        in_specs=[
            pl.BlockSpec((tm, D), lambda i: (i, 0)),
            pl.BlockSpec((tm, ATTN_WIDTH), lambda i: (i, 0)),
            pl.BlockSpec((tm, CONV_WIDTH), lambda i: (i, 0)),
            pl.BlockSpec(attn_out_norm.shape, const),
            pl.BlockSpec(w_o.shape, const),
            pl.BlockSpec(moe_norm.shape, const),
            pl.BlockSpec(w_router.shape, const),
            pl.BlockSpec(b_router.shape, const),
        ],
        out_specs=[
            pl.BlockSpec((tm, D), lambda i: (i, 0)),
            pl.BlockSpec((tm * SUBLANES, LANES), lambda i: (i, 0)),
            pl.BlockSpec((tm, 2 * TOP_K), lambda i: (i, 0)),
            pl.BlockSpec((tm, TOP_K), lambda i: (i, 0)),
            pl.BlockSpec((1, N_EXPERTS), const),
        ],
        out_shape=[
            jax.ShapeDtypeStruct((N, D), F32),
            jax.ShapeDtypeStruct((N * SUBLANES, LANES), F32),
            jax.ShapeDtypeStruct((N, 2 * TOP_K), jnp.int32),
            jax.ShapeDtypeStruct((N, TOP_K), F32),
            jax.ShapeDtypeStruct((1, N_EXPERTS), jnp.int32),
        ],
        scratch_shapes=[pltpu.VMEM((1, N_EXPERTS), F32)],
        compiler_params=pltpu.CompilerParams(
            dimension_semantics=("arbitrary",), vmem_limit_bytes=VMEM_LIMIT),
        name="outproj_router",
    )(x2, attn2, conv2, attn_out_norm, w_o, moe_norm, w_router, b_router)


def _row_copy(src_ref, src_row, dst_ref, dst_row, sem):
    s = pl.multiple_of(src_row * SUBLANES, SUBLANES)
    d = pl.multiple_of(dst_row * SUBLANES, SUBLANES)
    return pltpu.make_async_copy(src_ref.at[pl.ds(s, SUBLANES), :], dst_ref.at[pl.ds(d, SUBLANES), :], sem)


def _dispatch_kernel(fill_lo_ref, fill_hi_ref, dest_ref, hn_ref, zero_ref, xs_ref, sem, zsem):
    step = pl.program_id(0)

    @pl.when(step == 0)
    def _():
        def per_segment(e, _):
            def fill(r, _):
                _row_copy(zero_ref, 0, xs_ref, r, zsem).start()
                return 0
            lax.fori_loop(fill_lo_ref[e], fill_hi_ref[e], fill, 0)

            def drain(r, _):
                _row_copy(zero_ref, 0, xs_ref, r, zsem).wait()
                return 0
            lax.fori_loop(fill_lo_ref[e], fill_hi_ref[e], drain, 0)
            return 0
        lax.fori_loop(0, N_EXPERTS + 1, per_segment, 0)

    def copies(t):
        tok = step * TD + t
        return [_row_copy(hn_ref, tok, xs_ref, dest_ref[TOP_K * t + k], sem) for k in range(TOP_K)]

    def issue(t, _):
        for cp in copies(t):
            cp.start()
        return 0

    def drain(t, _):
        for cp in copies(t):
            cp.wait()
        return 0

    lax.fori_loop(0, TD, issue, 0)
    lax.fori_loop(0, TD, drain, 0)


def _dispatch(fill_lo, fill_hi, dest_flat, hn_rows, n_rows_padded):
    N = hn_rows.shape[0] // SUBLANES
    zero_row = jnp.zeros((SUBLANES, LANES), F32)
    grid_spec = pltpu.PrefetchScalarGridSpec(
        num_scalar_prefetch=2,
        grid=(N // TD,),
        in_specs=[
            pl.BlockSpec((TD * TOP_K,), lambda i, *_: (i,), memory_space=pltpu.SMEM),
            pl.BlockSpec(memory_space=pl.ANY),
            pl.BlockSpec(memory_space=pl.ANY),
        ],
        out_specs=pl.BlockSpec(memory_space=pl.ANY),
        scratch_shapes=[pltpu.SemaphoreType.DMA(()), pltpu.SemaphoreType.DMA(())],
    )
    return pl.pallas_call(
        _dispatch_kernel,
        grid_spec=grid_spec,
        out_shape=jax.ShapeDtypeStruct((n_rows_padded * SUBLANES, LANES), F32),
        compiler_params=pltpu.CompilerParams(
            dimension_semantics=("arbitrary",), has_side_effects=True),
        name="dispatch",
    )(fill_lo, fill_hi, dest_flat, hn_rows, zero_row)


def _ffn_kernel(blk_e_ref, nblk_ref, xs_ref, w1_ref, b1_ref, w2_ref, b2_ref, y_ref):
    i = pl.program_id(0)

    @pl.when(i < nblk_ref[0])
    def _():
        xb = jnp.concatenate(
            [xs_ref[pl.ds(j, TMF, stride=SUBLANES), :] for j in range(ROW_CHUNKS)], axis=1)
        h = jnp.dot(xb.astype(BF16), w1_ref[0], preferred_element_type=F32) + b1_ref[0]
        gate = jnp.minimum(h[:, :D_EXPERT], SWIGLU_LIMIT)
        up = jnp.clip(h[:, D_EXPERT:], -SWIGLU_LIMIT, SWIGLU_LIMIT)
        act = gate * jax.nn.sigmoid(SWIGLU_ALPHA * gate) * (up + 1.0)
        y = jnp.dot(act.astype(BF16), w2_ref[0], preferred_element_type=F32) + b2_ref[0]
        for j in range(ROW_CHUNKS):
            y_ref[pl.ds(j, TMF, stride=SUBLANES), :] = y[:, j * LANES:(j + 1) * LANES]

    @pl.when(i >= nblk_ref[0])
    def _():
        y_ref[...] = jnp.zeros_like(y_ref)


def _ffn(blk_expert, n_used, xs_rows, w1, b1, w2, b2):
    n_blocks = blk_expert.shape[0]

    def row_map(i, be, nb):
        return (i, 0)

    def w_map(i, be, nb):
        return (be[i], 0, 0)

    grid_spec = pltpu.PrefetchScalarGridSpec(
        num_scalar_prefetch=2,
        grid=(n_blocks,),
        in_specs=[
            pl.BlockSpec((TMF * SUBLANES, LANES), row_map),
            pl.BlockSpec((1, D_MODEL, 2 * D_EXPERT), w_map),
            pl.BlockSpec((1, 1, 2 * D_EXPERT), w_map),
            pl.BlockSpec((1, D_EXPERT, D_MODEL), w_map),
            pl.BlockSpec((1, 1, D_MODEL), w_map),
        ],
        out_specs=pl.BlockSpec((TMF * SUBLANES, LANES), row_map),
    )
    return pl.pallas_call(
        _ffn_kernel,
        grid_spec=grid_spec,
        out_shape=jax.ShapeDtypeStruct(xs_rows.shape, F32),
        compiler_params=pltpu.CompilerParams(
            dimension_semantics=("arbitrary",), vmem_limit_bytes=VMEM_LIMIT),
        name="expert_ffn",
    )(blk_expert, n_used, xs_rows, w1, b1, w2, b2)


def _final_kernel(dest_ref, h1_ref, gate_ref, p_ref, y_ref, gple_ref, wg_ref, wp_ref, gpost_ref, gfin_ref,
                  out_ref, ybuf_ref, sem):
    def copies(t):
        return [_row_copy(y_ref, dest_ref[TOP_K * t + k], ybuf_ref, k * TF + t, sem) for k in range(TOP_K)]

    def issue(t, _):
        for cp in copies(t):
            cp.start()
        return 0

    def drain(t, _):
        for cp in copies(t):
            cp.wait()
        return 0

    lax.fori_loop(0, TF, issue, 0)
    lax.fori_loop(0, TF, drain, 0)

    g = gate_ref[...]
    cols = []
    for j in range(ROW_CHUNKS):
        acc = jnp.zeros((TF, LANES), F32)
        for k in range(TOP_K):
            rows = ybuf_ref[pl.ds(k * TF * SUBLANES + j, TF, stride=SUBLANES), :]
            acc = acc + g[:, k:k + 1] * rows
        cols.append(acc)
    h2 = h1_ref[...] + jnp.concatenate(cols, axis=1)

    gate = jax.nn.sigmoid(jnp.dot(_rms(h2, gple_ref[...]).astype(BF16), wg_ref[...],
                                  preferred_element_type=F32))
    ple = _rms(jnp.dot(p_ref[...].astype(BF16), wp_ref[...], preferred_element_type=F32), gpost_ref[...])
    out_ref[...] = _rms(h2 + gate * ple, gfin_ref[...])


def _final(dest_flat, h1, gates, p2, y_rows, ple_norm, w_ple_gate, w_ple_proj, ple_post_norm, final_norm):
    N, D = h1.shape
    const = lambda i: (0, 0)
    return pl.pallas_call(
        _final_kernel,
        grid=(N // TF,),
        in_specs=[
            pl.BlockSpec((TF * TOP_K,), lambda i: (i,), memory_space=pltpu.SMEM),
            pl.BlockSpec((TF, D), lambda i: (i, 0)),
            pl.BlockSpec((TF, TOP_K), lambda i: (i, 0)),
            pl.BlockSpec((TF, PLE_DIM), lambda i: (i, 0)),
            pl.BlockSpec(memory_space=pl.ANY),
            pl.BlockSpec(ple_norm.shape, const),
            pl.BlockSpec(w_ple_gate.shape, const),
            pl.BlockSpec(w_ple_proj.shape, const),
            pl.BlockSpec(ple_post_norm.shape, const),
            pl.BlockSpec(final_norm.shape, const),
        ],
        out_specs=pl.BlockSpec((TF, D), lambda i: (i, 0)),
        out_shape=jax.ShapeDtypeStruct((N, D), F32),
        scratch_shapes=[pltpu.VMEM((TOP_K * TF * SUBLANES, LANES), F32), pltpu.SemaphoreType.DMA(())],
        compiler_params=pltpu.CompilerParams(
            dimension_semantics=("arbitrary",), vmem_limit_bytes=VMEM_LIMIT),
        name="combine_ple_final",
    )(dest_flat, h1, gates, p2, y_rows, ple_norm, w_ple_gate, w_ple_proj, ple_post_norm, final_norm)


def _reorder_w_in(w):
    o = Q_LORA_RANK + KV_LORA_RANK
    return jnp.concatenate([w[:, :o], w[:, o + QK_ROPE_DIM:], w[:, o:o + QK_ROPE_DIM]], axis=1)


def _reorder_w_uq(w):
    w4 = w.reshape(Q_LORA_RANK, N_HEADS, QK_DIM)
    nope = w4[:, :, :QK_NOPE_DIM].reshape(Q_LORA_RANK, N_HEADS * QK_NOPE_DIM)
    r1 = w4[:, :, QK_NOPE_DIM:QK_NOPE_DIM + ROPE_HALF].reshape(Q_LORA_RANK, N_HEADS * ROPE_HALF)
    r2 = w4[:, :, QK_NOPE_DIM + ROPE_HALF:].reshape(Q_LORA_RANK, N_HEADS * ROPE_HALF)
    return jnp.concatenate([nope, r1, r2], axis=1)


def kernel(x, p, positions, mix_norm, w_in, q_norm, w_uq, kv_norm, w_ukv, conv_w, attn_out_norm, conv_out_norm, w_o, moe_norm, w_router, b_router, w1, b1, w2, b2, ple_norm, w_ple_gate, w_ple_proj, ple_post_norm, final_norm):
    B, S, D = x.shape
    N = B * S
    assert p.shape[0] == 1, "single layer"
    half = ROPE_HALF
    inv_freq = ROPE_THETA ** (-jnp.arange(half, dtype=F32) / half)
    invf4 = jnp.tile(inv_freq, LANES // half)[None, :]

    q, k, v, conv_n = _inproj(
        x, positions[:, :, None], invf4, mix_norm, _reorder_w_in(w_in[0]).astype(BF16), q_norm,
        _reorder_w_uq(w_uq[0]).astype(BF16), kv_norm, w_ukv[0].astype(BF16), conv_w[0], conv_out_norm)
    attn = _attention(q, k, v)

    h1, hn_rows, route, gates, counts = _outproj(
        x.reshape(N, D), attn.reshape(N, ATTN_WIDTH), conv_n.reshape(N, CONV_WIDTH), attn_out_norm,
        w_o[0].astype(BF16), moe_norm, w_router[0], b_router)

    counts = counts[0]
    padded = ((counts + TMF - 1) // TMF) * TMF
    pad_ends = jnp.cumsum(padded)
    pad_starts = pad_ends - padded
    n_blocks = (N * TOP_K) // TMF + N_EXPERTS
    dest = pad_starts[route[:, :TOP_K]] + route[:, TOP_K:]
    dest_flat = dest.reshape(N * TOP_K).astype(jnp.int32)
    blk_expert = jnp.clip(
        jnp.searchsorted(pad_ends, jnp.arange(n_blocks, dtype=jnp.int32) * TMF, side="right"),
        0, N_EXPERTS - 1).astype(jnp.int32)
    n_used = (pad_ends[-1:] // TMF).astype(jnp.int32)

    total_rows = jnp.full((1,), n_blocks * TMF, jnp.int32)
    fill_lo = jnp.concatenate([pad_starts + counts, pad_ends[-1:]]).astype(jnp.int32)
    fill_hi = jnp.concatenate([pad_ends, total_rows]).astype(jnp.int32)
    xs_rows = _dispatch(fill_lo, fill_hi, dest_flat, hn_rows, n_blocks * TMF)
    y_rows = _ffn(blk_expert, n_used, xs_rows, w1[0].astype(BF16), b1[0][:, None, :],
                  w2[0].astype(BF16), b2[0][:, None, :])

    out = _final(dest_flat, h1, gates, p[0].reshape(N, PLE_DIM), y_rows, ple_norm,
                 w_ple_gate[0].astype(BF16), w_ple_proj[0].astype(BF16), ple_post_norm,
                 final_norm.reshape(1, D))
    return out.reshape(B, S, D)
```

```python
import functools

import jax
import jax.numpy as jnp
from jax import lax
from jax.experimental import pallas as pl
from jax.experimental.pallas import tpu as pltpu

D_MODEL = 1024
PLE_DIM = 256
N_HEADS = 4
QK_NOPE_DIM = 128
QK_ROPE_DIM = 64
ROPE_HALF = QK_ROPE_DIM // 2
V_HEAD_DIM = 128
V_EXT = 2 * V_HEAD_DIM
QK_DIM = QK_NOPE_DIM + QK_ROPE_DIM
Q_LORA_RANK = 256
KV_LORA_RANK = 128
ROPE_THETA = 10000.0
ATTN_WIDTH = N_HEADS * V_HEAD_DIM
CONV_WIDTH = 512
CONV_K = 3
N_EXPERTS = 32
TOP_K = 4
D_EXPERT = 1024
SWIGLU_ALPHA = 1.702
SWIGLU_LIMIT = 7.0
EPS = 1e-6

SUBLANES = 8
LANES = 128
ROW_CHUNKS = D_MODEL // LANES

TM_IN = 512
TQ = 512
TK = 512
TM_OUT = 512
TD = 256
TMF = 256
TF = 256

VMEM_LIMIT = 56 * 1024 * 1024

BF16 = jnp.bfloat16
F32 = jnp.float32


def _rms(x, g):
    return x * lax.rsqrt(jnp.mean(x * x, axis=-1, keepdims=True) + EPS) * g


def _inproj_kernel(x_ref, pos_ref, invf_ref, gmix_ref, win_ref, gq_ref, wuq_ref, gkv_ref, wukv_ref,
                   convw_ref, gconv_ref, q_ref, k_ref, v_ref, conv_ref, ubuf_ref):
    tm = x_ref.shape[1]
    s_idx = pl.program_id(1)

    xn = _rms(x_ref[0], gmix_ref[...])
    z = jnp.dot(xn.astype(BF16), win_ref[...], preferred_element_type=F32)
    c_q = z[:, 0:Q_LORA_RANK]
    c_kv = z[:, Q_LORA_RANK:Q_LORA_RANK + KV_LORA_RANK]
    o = Q_LORA_RANK + KV_LORA_RANK
    b_gate = z[:, o:o + CONV_WIDTH]
    c_gate = z[:, o + CONV_WIDTH:o + 2 * CONV_WIDTH]
    val = z[:, o + 2 * CONV_WIDTH:o + 3 * CONV_WIDTH]
    k_pe = z[:, o + 3 * CONV_WIDTH:o + 3 * CONV_WIDTH + QK_ROPE_DIM]

    ang = pos_ref[0].astype(F32) * invf_ref[...]
    cos4 = jnp.cos(ang)
    sin4 = jnp.sin(ang)

    q = jnp.dot(_rms(c_q, gq_ref[...]).astype(BF16), wuq_ref[...], preferred_element_type=F32)
    qn = N_HEADS * QK_NOPE_DIM
    x1 = q[:, qn:qn + LANES]
    x2 = q[:, qn + LANES:qn + 2 * LANES]
    o1 = x1 * cos4 - x2 * sin4
    o2 = x1 * sin4 + x2 * cos4
    scale = QK_DIM ** -0.5

    kv = jnp.dot(_rms(c_kv, gkv_ref[...]).astype(BF16), wukv_ref[...], preferred_element_type=F32)
    k1 = k_pe[:, :ROPE_HALF]
    k2 = k_pe[:, ROPE_HALF:]
    c1 = cos4[:, :ROPE_HALF]
    s1 = sin4[:, :ROPE_HALF]
    kr = jnp.concatenate([k1 * c1 - k2 * s1, k1 * s1 + k2 * c1], axis=1)

    for h in range(N_HEADS):
        qh = jnp.concatenate(
            [q[:, h * QK_NOPE_DIM:(h + 1) * QK_NOPE_DIM],
             o1[:, h * ROPE_HALF:(h + 1) * ROPE_HALF],
             o2[:, h * ROPE_HALF:(h + 1) * ROPE_HALF]], axis=1) * scale
        q_ref[0, h] = qh.astype(BF16)
        base = h * (QK_NOPE_DIM + V_HEAD_DIM)
        kh = jnp.concatenate([kv[:, base:base + QK_NOPE_DIM], kr], axis=1)
        k_ref[0, h] = kh.astype(BF16)
        vh = kv[:, base + QK_NOPE_DIM:base + QK_NOPE_DIM + V_HEAD_DIM]
        v_ref[0, h] = jnp.concatenate([vh, jnp.ones_like(vh)], axis=1).astype(BF16)

    u = c_gate * val

    @pl.when(s_idx == 0)
    def _():
        ubuf_ref[0:SUBLANES, :] = jnp.zeros((SUBLANES, CONV_WIDTH), F32)

    ubuf_ref[SUBLANES:SUBLANES + tm, :] = u
    u_m1 = ubuf_ref[SUBLANES - 1:SUBLANES - 1 + tm, :]
    u_m2 = ubuf_ref[SUBLANES - 2:SUBLANES - 2 + tm, :]
    cw = convw_ref[...]
    y = cw[0:1, :] * u_m2 + cw[1:2, :] * u_m1 + cw[2:3, :] * u
    conv_ref[0] = _rms(b_gate * y, gconv_ref[...]).astype(BF16)
    ubuf_ref[0:SUBLANES, :] = ubuf_ref[tm:tm + SUBLANES, :]


def _inproj(x, positions, invf4, mix_norm, w_in_r, q_norm, w_uq_r, kv_norm, w_ukv, conv_w, conv_out_norm):
    B, S, D = x.shape
    tm = TM_IN
    const = lambda b, s: (0, 0)
    return pl.pallas_call(
        _inproj_kernel,
        grid=(B, S // tm),
        in_specs=[
            pl.BlockSpec((1, tm, D), lambda b, s: (b, s, 0)),
            pl.BlockSpec((1, tm, 1), lambda b, s: (b, s, 0)),
            pl.BlockSpec(invf4.shape, const),
            pl.BlockSpec(mix_norm.shape, const),
            pl.BlockSpec(w_in_r.shape, const),
            pl.BlockSpec(q_norm.shape, const),
            pl.BlockSpec(w_uq_r.shape, const),
            pl.BlockSpec(kv_norm.shape, const),
            pl.BlockSpec(w_ukv.shape, const),
            pl.BlockSpec(conv_w.shape, const),
            pl.BlockSpec(conv_out_norm.shape, const),
        ],
        out_specs=[
            pl.BlockSpec((1, N_HEADS, tm, QK_DIM), lambda b, s: (b, 0, s, 0)),
            pl.BlockSpec((1, N_HEADS, tm, QK_DIM), lambda b, s: (b, 0, s, 0)),
            pl.BlockSpec((1, N_HEADS, tm, V_EXT), lambda b, s: (b, 0, s, 0)),
            pl.BlockSpec((1, tm, CONV_WIDTH), lambda b, s: (b, s, 0)),
        ],
        out_shape=[
            jax.ShapeDtypeStruct((B, N_HEADS, S, QK_DIM), BF16),
            jax.ShapeDtypeStruct((B, N_HEADS, S, QK_DIM), BF16),
            jax.ShapeDtypeStruct((B, N_HEADS, S, V_EXT), BF16),
            jax.ShapeDtypeStruct((B, S, CONV_WIDTH), BF16),
        ],
        scratch_shapes=[pltpu.VMEM((tm + 2 * SUBLANES, CONV_WIDTH), F32)],
        compiler_params=pltpu.CompilerParams(
            dimension_semantics=("arbitrary", "arbitrary"), vmem_limit_bytes=VMEM_LIMIT),
        name="inproj",
    )(x, positions, invf4, mix_norm, w_in_r, q_norm, w_uq_r, kv_norm, w_ukv, conv_w, conv_out_norm)


def _attn_kernel(q_ref, k_ref, v_ref, o_ref):
    tq = q_ref.shape[2]
    i = pl.program_id(1)
    neg = jnp.float32(-1e30)

    def step(j, carry, masked):
        start = pl.multiple_of(j * TK, TK)
        if masked:
            row = lax.broadcasted_iota(jnp.int32, (tq, TK), 0)
            col = lax.broadcasted_iota(jnp.int32, (tq, TK), 1)
            keep = col <= row
        new = []
        for h in range(N_HEADS):
            m, acc = carry[h]
            kj = k_ref[0, h, pl.ds(start, TK), :]
            vj = v_ref[0, h, pl.ds(start, TK), :]
            s = lax.dot_general(q_ref[0, h], kj, (((1,), (1,)), ((), ())), preferred_element_type=F32)
            if masked:
                s = jnp.where(keep, s, neg)
            m_new = jnp.maximum(m, jnp.max(s, axis=-1, keepdims=True))
            alpha = jnp.exp(m - m_new)
            p = jnp.exp(s - m_new)
            acc = alpha * acc + jnp.dot(p.astype(BF16), vj, preferred_element_type=F32)
            new.append((m_new, acc))
        return tuple(new)

    init = tuple((jnp.full((tq, 1), neg, F32), jnp.zeros((tq, V_EXT), F32)) for _ in range(N_HEADS))
    carry = lax.fori_loop(0, i, lambda j, c: step(j, c, False), init)
    final = step(i, carry, True)
    for h in range(N_HEADS):
        acc = final[h][1]
        o_ref[0, :, h * V_HEAD_DIM:(h + 1) * V_HEAD_DIM] = (
            acc[:, :V_HEAD_DIM] / acc[:, V_HEAD_DIM:V_HEAD_DIM + 1]).astype(BF16)


def _attention(q, k, v):
    B, H, S, _ = q.shape
    assert TQ == TK
    return pl.pallas_call(
        _attn_kernel,
        grid=(B, S // TQ),
        in_specs=[
            pl.BlockSpec((1, H, TQ, QK_DIM), lambda b, i: (b, 0, i, 0)),
            pl.BlockSpec((1, H, S, QK_DIM), lambda b, i: (b, 0, 0, 0)),
            pl.BlockSpec((1, H, S, V_EXT), lambda b, i: (b, 0, 0, 0)),
        ],
        out_specs=pl.BlockSpec((1, TQ, H * V_HEAD_DIM), lambda b, i: (b, i, 0)),
        out_shape=jax.ShapeDtypeStruct((B, S, H * V_HEAD_DIM), BF16),
        compiler_params=pltpu.CompilerParams(
            dimension_semantics=("arbitrary", "arbitrary"), vmem_limit_bytes=VMEM_LIMIT),
        name="attention",
    )(q, k, v)


def _outproj_kernel(x_ref, attn_ref, conv_ref, ga_ref, wo_ref, gmoe_ref, wr_ref, br_ref,
                    h1_ref, hn_ref, route_ref, gate_ref, counts_ref, carry_ref):
    tm = x_ref.shape[0]
    step = pl.program_id(0)

    @pl.when(step == 0)
    def _():
        carry_ref[...] = jnp.zeros_like(carry_ref)

    attn_n = _rms(attn_ref[...].astype(F32), ga_ref[...]).astype(BF16)
    mixed = jnp.concatenate([attn_n, conv_ref[...]], axis=1)
    h1 = x_ref[...] + jnp.dot(mixed, wo_ref[...], preferred_element_type=F32)
    h1_ref[...] = h1

    hn = _rms(h1, gmoe_ref[...])
    for j in range(ROW_CHUNKS):
        hn_ref[pl.ds(j, tm, stride=SUBLANES), :] = hn[:, j * LANES:(j + 1) * LANES]

    logits = jnp.dot(hn, wr_ref[...], preferred_element_type=F32,
                     precision=lax.Precision.HIGHEST) + br_ref[...]

    lane = lax.broadcasted_iota(jnp.int32, (tm, N_EXPERTS), 1).astype(F32)
    work = logits
    vals, idxs, hots = [], [], []
    for _ in range(TOP_K):
        mx = jnp.max(work, axis=-1, keepdims=True)
        idx = jnp.min(jnp.where(work == mx, lane, float(N_EXPERTS)), axis=-1, keepdims=True)
        hot = lane == idx
        work = jnp.where(hot, -jnp.inf, work)
        vals.append(mx)
        idxs.append(idx)
        hots.append(hot)

    exps = [jnp.exp(v - vals[0]) for v in vals]
    denom = exps[0] + exps[1] + exps[2] + exps[3]
    gates = [e / denom for e in exps]

    sel = (hots[0] | hots[1] | hots[2] | hots[3]).astype(F32)
    r = lax.broadcasted_iota(jnp.int32, (tm, tm), 0)
    c = lax.broadcasted_iota(jnp.int32, (tm, tm), 1)
    tri = (c < r).astype(BF16)
    rank_mat = jnp.dot(tri, sel.astype(BF16), preferred_element_type=F32) + carry_ref[...]
    carry_ref[...] = carry_ref[...] + jnp.sum(sel, axis=0, keepdims=True)
    counts_ref[...] = carry_ref[...].astype(jnp.int32)

    ranks = [jnp.sum(jnp.where(h, rank_mat, 0.0), axis=-1, keepdims=True) for h in hots]
    lane8 = lax.broadcasted_iota(jnp.int32, (tm, 2 * TOP_K), 1)
    route = jnp.zeros((tm, 2 * TOP_K), F32)
    gate_out = jnp.zeros((tm, TOP_K), F32)
    lane4 = lax.broadcasted_iota(jnp.int32, (tm, TOP_K), 1)
    for k in range(TOP_K):
        route = jnp.where(lane8 == k, idxs[k], route)
        route = jnp.where(lane8 == TOP_K + k, ranks[k], route)
        gate_out = jnp.where(lane4 == k, gates[k], gate_out)
    route_ref[...] = route.astype(jnp.int32)
    gate_ref[...] = gate_out


def _outproj(x2, attn2, conv2, attn_out_norm, w_o, moe_norm, w_router, b_router):
    N, D = x2.shape
    tm = TM_OUT
    const = lambda i: (0, 0)
    return pl.pallas_call(
        _outproj_kernel,
        grid=(N // tm,),
        in_specs=[
            pl.BlockSpec((tm, D), lambda i: (i, 0)),
            pl.BlockSpec((tm, ATTN_WIDTH), lambda i: (i, 0)),
            pl.BlockSpec((tm, CONV_WIDTH), lambda i: (i, 0)),
            pl.BlockSpec(attn_out_norm.shape, const),
            pl.BlockSpec(w_o.shape, const),
            pl.BlockSpec(moe_norm.shape, const),
            pl.BlockSpec(w_router.shape, const),
            pl.BlockSpec(b_router.shape, const),
        ],
        out_specs=[
            pl.BlockSpec((tm, D), lambda i: (i, 0)),
            pl.BlockSpec((tm * SUBLANES, LANES), lambda i: (i, 0)),
            pl.BlockSpec((tm, 2 * TOP_K), lambda i: (i, 0)),
            pl.BlockSpec((tm, TOP_K), lambda i: (i, 0)),
            pl.BlockSpec((1, N_EXPERTS), const),
        ],
        out_shape=[
            jax.ShapeDtypeStruct((N, D), F32),
            jax.ShapeDtypeStruct((N * SUBLANES, LANES), F32),
            jax.ShapeDtypeStruct((N, 2 * TOP_K), jnp.int32),
            jax.ShapeDtypeStruct((N, TOP_K), F32),
            jax.ShapeDtypeStruct((1, N_EXPERTS), jnp.int32),
        ],
        scratch_shapes=[pltpu.VMEM((1, N_EXPERTS), F32)],
        compiler_params=pltpu.CompilerParams(
            dimension_semantics=("arbitrary",), vmem_limit_bytes=VMEM_LIMIT),
        name="outproj_router",
    )(x2, attn2, conv2, attn_out_norm, w_o, moe_norm, w_router, b_router)


def _row_copy(src_ref, src_row, dst_ref, dst_row, sem):
    s = pl.multiple_of(src_row * SUBLANES, SUBLANES)
    d = pl.multiple_of(dst_row * SUBLANES, SUBLANES)
    return pltpu.make_async_copy(src_ref.at[pl.ds(s, SUBLANES), :], dst_ref.at[pl.ds(d, SUBLANES), :], sem)


def _dispatch_kernel(fill_lo_ref, fill_hi_ref, dest_ref, hn_ref, zero_ref, xs_ref, sem, zsem):
    step = pl.program_id(0)

    @pl.when(step == 0)
    def _():
        def per_segment(e, _):
            def fill(r, _):
                _row_copy(zero_ref, 0, xs_ref, r, zsem).start()
                return 0
            lax.fori_loop(fill_lo_ref[e], fill_hi_ref[e], fill, 0)

            def drain(r, _):
                _row_copy(zero_ref, 0, xs_ref, r, zsem).wait()
                return 0
            lax.fori_loop(fill_lo_ref[e], fill_hi_ref[e], drain, 0)
            return 0
        lax.fori_loop(0, N_EXPERTS + 1, per_segment, 0)

    def copies(t):
        return [_row_copy(hn_ref, t, xs_ref, dest_ref[TOP_K * t + k], sem) for k in range(TOP_K)]

    def issue(t, _):
        for cp in copies(t):
            cp.start()
        return 0

    def drain(t, _):
        for cp in copies(t):
            cp.wait()
        return 0

    lax.fori_loop(0, TD, issue, 0)
    lax.fori_loop(0, TD, drain, 0)


def _dispatch(fill_lo, fill_hi, dest_flat, hn_rows, n_rows_padded):
    N = hn_rows.shape[0] // SUBLANES
    zero_row = jnp.zeros((SUBLANES, LANES), F32)
    grid_spec = pltpu.PrefetchScalarGridSpec(
        num_scalar_prefetch=2,
        grid=(N // TD,),
        in_specs=[
            pl.BlockSpec((TD * TOP_K,), lambda i, *_: (i,), memory_space=pltpu.SMEM),
            pl.BlockSpec((TD * SUBLANES, LANES), lambda i, *_: (i, 0)),
            pl.BlockSpec((SUBLANES, LANES), lambda i, *_: (0, 0)),
        ],
        out_specs=pl.BlockSpec(memory_space=pl.ANY),
        scratch_shapes=[pltpu.SemaphoreType.DMA(()), pltpu.SemaphoreType.DMA(())],
    )
    return pl.pallas_call(
        _dispatch_kernel,
        grid_spec=grid_spec,
        out_shape=jax.ShapeDtypeStruct((n_rows_padded * SUBLANES, LANES), F32),
        compiler_params=pltpu.CompilerParams(
            dimension_semantics=("arbitrary",), has_side_effects=True),
        name="dispatch",
    )(fill_lo, fill_hi, dest_flat, hn_rows, zero_row)


def _ffn_kernel(blk_e_ref, nblk_ref, xs_ref, w1_ref, b1_ref, w2_ref, b2_ref, y_ref):
    i = pl.program_id(0)

    @pl.when(i < nblk_ref[0])
    def _():
        xb = jnp.concatenate(
            [xs_ref[pl.ds(j, TMF, stride=SUBLANES), :] for j in range(ROW_CHUNKS)], axis=1)
        h = jnp.dot(xb.astype(BF16), w1_ref[0], preferred_element_type=F32) + b1_ref[0]
        gate = jnp.minimum(h[:, :D_EXPERT], SWIGLU_LIMIT)
        up = jnp.clip(h[:, D_EXPERT:], -SWIGLU_LIMIT, SWIGLU_LIMIT)
        act = gate * jax.nn.sigmoid(SWIGLU_ALPHA * gate) * (up + 1.0)
        y = jnp.dot(act.astype(BF16), w2_ref[0], preferred_element_type=F32) + b2_ref[0]
        for j in range(ROW_CHUNKS):
            y_ref[pl.ds(j, TMF, stride=SUBLANES), :] = y[:, j * LANES:(j + 1) * LANES]

    @pl.when(i >= nblk_ref[0])
    def _():
        y_ref[...] = jnp.zeros_like(y_ref)


def _ffn(blk_expert, n_used, xs_rows, w1, b1, w2, b2):
    n_blocks = blk_expert.shape[0]

    def row_map(i, be, nb):
        return (i, 0)

    def w_map(i, be, nb):
        return (be[i], 0, 0)

    grid_spec = pltpu.PrefetchScalarGridSpec(
        num_scalar_prefetch=2,
        grid=(n_blocks,),
        in_specs=[
            pl.BlockSpec((TMF * SUBLANES, LANES), row_map),
            pl.BlockSpec((1, D_MODEL, 2 * D_EXPERT), w_map),
            pl.BlockSpec((1, 1, 2 * D_EXPERT), w_map),
            pl.BlockSpec((1, D_EXPERT, D_MODEL), w_map),
            pl.BlockSpec((1, 1, D_MODEL), w_map),
        ],
        out_specs=pl.BlockSpec((TMF * SUBLANES, LANES), row_map),
    )
    return pl.pallas_call(
        _ffn_kernel,
        grid_spec=grid_spec,
        out_shape=jax.ShapeDtypeStruct(xs_rows.shape, F32),
        compiler_params=pltpu.CompilerParams(
            dimension_semantics=("arbitrary",), vmem_limit_bytes=VMEM_LIMIT),
        name="expert_ffn",
    )(blk_expert, n_used, xs_rows, w1, b1, w2, b2)


def _final_kernel(dest_ref, dest_next_ref, h1_ref, gate_ref, p_ref, y_ref, gple_ref, wg_ref, wp_ref,
                  gpost_ref, gfin_ref, out_ref, ybuf0_ref, ybuf1_ref, sems):
    step = pl.program_id(0)
    n_steps = pl.num_programs(0)
    bufs = (ybuf0_ref, ybuf1_ref)

    def gather(idx_ref, slot, start):
        def body(t, _):
            for k in range(TOP_K):
                cp = _row_copy(y_ref, idx_ref[TOP_K * t + k], bufs[slot], k * TF + t, sems.at[slot])
                if start:
                    cp.start()
                else:
                    cp.wait()
            return 0
        lax.fori_loop(0, TF, body, 0)

    def compute(slot):
        g = gate_ref[...]
        cols = []
        for j in range(ROW_CHUNKS):
            acc = jnp.zeros((TF, LANES), F32)
            for k in range(TOP_K):
                rows = bufs[slot][pl.ds(k * TF * SUBLANES + j, TF, stride=SUBLANES), :]
                acc = acc + g[:, k:k + 1] * rows
            cols.append(acc)
        h2 = h1_ref[...] + jnp.concatenate(cols, axis=1)
        gate = jax.nn.sigmoid(jnp.dot(_rms(h2, gple_ref[...]).astype(BF16), wg_ref[...],
                                      preferred_element_type=F32))
        ple = _rms(jnp.dot(p_ref[...].astype(BF16), wp_ref[...], preferred_element_type=F32),
                   gpost_ref[...])
        out_ref[...] = _rms(h2 + gate * ple, gfin_ref[...])

    @pl.when(step == 0)
    def _():
        gather(dest_ref, 0, start=True)

    for slot in range(2):
        @pl.when(step % 2 == slot)
        def _(slot=slot):
            @pl.when(step + 1 < n_steps)
            def _():
                gather(dest_next_ref, 1 - slot, start=True)
            gather(dest_ref, slot, start=False)
            compute(slot)


def _final(dest_flat, h1, gates, p2, y_rows, ple_norm, w_ple_gate, w_ple_proj, ple_post_norm, final_norm):
    N, D = h1.shape
    const = lambda i: (0, 0)
    n_steps = N // TF
    return pl.pallas_call(
        _final_kernel,
        grid=(n_steps,),
        in_specs=[
            pl.BlockSpec((TF * TOP_K,), lambda i: (i,), memory_space=pltpu.SMEM),
            pl.BlockSpec((TF * TOP_K,), lambda i: (jnp.minimum(i + 1, n_steps - 1),),
                         memory_space=pltpu.SMEM),
            pl.BlockSpec((TF, D), lambda i: (i, 0)),
            pl.BlockSpec((TF, TOP_K), lambda i: (i, 0)),
            pl.BlockSpec((TF, PLE_DIM), lambda i: (i, 0)),
            pl.BlockSpec(memory_space=pl.ANY),
            pl.BlockSpec(ple_norm.shape, const),
            pl.BlockSpec(w_ple_gate.shape, const),
            pl.BlockSpec(w_ple_proj.shape, const),
            pl.BlockSpec(ple_post_norm.shape, const),
            pl.BlockSpec(final_norm.shape, const),
        ],
        out_specs=pl.BlockSpec((TF, D), lambda i: (i, 0)),
        out_shape=jax.ShapeDtypeStruct((N, D), F32),
        scratch_shapes=[pltpu.VMEM((TOP_K * TF * SUBLANES, LANES), F32),
                        pltpu.VMEM((TOP_K * TF * SUBLANES, LANES), F32),
                        pltpu.SemaphoreType.DMA((2,))],
        compiler_params=pltpu.CompilerParams(
            dimension_semantics=("arbitrary",), vmem_limit_bytes=VMEM_LIMIT),
        name="combine_ple_final",
    )(dest_flat, dest_flat, h1, gates, p2, y_rows, ple_norm, w_ple_gate, w_ple_proj, ple_post_norm,
      final_norm)


def _reorder_w_in(w):
    o = Q_LORA_RANK + KV_LORA_RANK
    return jnp.concatenate([w[:, :o], w[:, o + QK_ROPE_DIM:], w[:, o:o + QK_ROPE_DIM]], axis=1)


def _reorder_w_uq(w):
    w4 = w.reshape(Q_LORA_RANK, N_HEADS, QK_DIM)
    nope = w4[:, :, :QK_NOPE_DIM].reshape(Q_LORA_RANK, N_HEADS * QK_NOPE_DIM)
    r1 = w4[:, :, QK_NOPE_DIM:QK_NOPE_DIM + ROPE_HALF].reshape(Q_LORA_RANK, N_HEADS * ROPE_HALF)
    r2 = w4[:, :, QK_NOPE_DIM + ROPE_HALF:].reshape(Q_LORA_RANK, N_HEADS * ROPE_HALF)
    return jnp.concatenate([nope, r1, r2], axis=1)


def kernel(x, p, positions, mix_norm, w_in, q_norm, w_uq, kv_norm, w_ukv, conv_w, attn_out_norm, conv_out_norm, w_o, moe_norm, w_router, b_router, w1, b1, w2, b2, ple_norm, w_ple_gate, w_ple_proj, ple_post_norm, final_norm):
    B, S, D = x.shape
    N = B * S
    assert p.shape[0] == 1, "single layer"
    half = ROPE_HALF
    inv_freq = ROPE_THETA ** (-jnp.arange(half, dtype=F32) / half)
    invf4 = jnp.tile(inv_freq, LANES // half)[None, :]

    q, k, v, conv_n = _inproj(
        x, positions[:, :, None], invf4, mix_norm, _reorder_w_in(w_in[0]).astype(BF16), q_norm,
        _reorder_w_uq(w_uq[0]).astype(BF16), kv_norm, w_ukv[0].astype(BF16), conv_w[0], conv_out_norm)
    attn = _attention(q, k, v)

    h1, hn_rows, route, gates, counts = _outproj(
        x.reshape(N, D), attn.reshape(N, ATTN_WIDTH), conv_n.reshape(N, CONV_WIDTH), attn_out_norm,
        w_o[0].astype(BF16), moe_norm, w_router[0], b_router)

    counts = counts[0]
    padded = ((counts + TMF - 1) // TMF) * TMF
    pad_ends = jnp.cumsum(padded)
    pad_starts = pad_ends - padded
    n_blocks = (N * TOP_K) // TMF + N_EXPERTS
    dest = pad_starts[route[:, :TOP_K]] + route[:, TOP_K:]
    dest_flat = dest.reshape(N * TOP_K).astype(jnp.int32)
    blk_start = jnp.arange(n_blocks, dtype=jnp.int32) * TMF
    blk_expert = jnp.minimum(
        jnp.sum((pad_ends[None, :] <= blk_start[:, None]).astype(jnp.int32), axis=1), N_EXPERTS - 1)
    n_used = (pad_ends[-1:] // TMF).astype(jnp.int32)

    total_rows = jnp.full((1,), n_blocks * TMF, jnp.int32)
    fill_lo = jnp.concatenate([pad_starts + counts, pad_ends[-1:]]).astype(jnp.int32)
    fill_hi = jnp.concatenate([pad_ends, total_rows]).astype(jnp.int32)
    xs_rows = _dispatch(fill_lo, fill_hi, dest_flat, hn_rows, n_blocks * TMF)
    y_rows = _ffn(blk_expert, n_used, xs_rows, w1[0].astype(BF16), b1[0][:, None, :],
                  w2[0].astype(BF16), b2[0][:, None, :])

    out = _final(dest_flat, h1, gates, p[0].reshape(N, PLE_DIM), y_rows, ple_norm,
                 w_ple_gate[0].astype(BF16), w_ple_proj[0].astype(BF16), ple_post_norm,
                 final_norm.reshape(1, D))
    return out.reshape(B, S, D)
```

```python
import functools

import jax
import jax.numpy as jnp
from jax import lax
from jax.experimental import pallas as pl
from jax.experimental.pallas import tpu as pltpu

D_MODEL = 1024
PLE_DIM = 256
N_HEADS = 4
QK_NOPE_DIM = 128
QK_ROPE_DIM = 64
ROPE_HALF = QK_ROPE_DIM // 2
V_HEAD_DIM = 128
V_EXT = 2 * V_HEAD_DIM
QK_DIM = QK_NOPE_DIM + QK_ROPE_DIM
Q_LORA_RANK = 256
KV_LORA_RANK = 128
ROPE_THETA = 10000.0
ATTN_WIDTH = N_HEADS * V_HEAD_DIM
CONV_WIDTH = 512
CONV_K = 3
N_EXPERTS = 32
TOP_K = 4
D_EXPERT = 1024
SWIGLU_ALPHA = 1.702
SWIGLU_LIMIT = 7.0
EPS = 1e-6

SUBLANES = 8
LANES = 128
ROW_CHUNKS = D_MODEL // LANES

TM_IN = 512
TQ = 512
TK = 512
ATT_ROW_SPLIT = 2
TM_OUT = 512
TD = 512
DISPATCH_CHUNK = 64
TMF = 512
W_CAST_ROWS = 256
TF = 256

VMEM_LIMIT = 56 * 1024 * 1024

BF16 = jnp.bfloat16
F32 = jnp.float32


def _rms(x, g):
    return x * lax.rsqrt(jnp.mean(x * x, axis=-1, keepdims=True) + EPS) * g


def _inproj_kernel(x_ref, pos_ref, invf_ref, gmix_ref, win_ref, gq_ref, wuq_ref, gkv_ref, wukv_ref,
                   convw_ref, gconv_ref, q_ref, k_ref, v_ref, conv_ref, ubuf_ref):
    tm = x_ref.shape[1]
    s_idx = pl.program_id(1)

    xn = _rms(x_ref[0], gmix_ref[...])
    z = jnp.dot(xn.astype(BF16), win_ref[...], preferred_element_type=F32)
    c_q = z[:, 0:Q_LORA_RANK]
    c_kv = z[:, Q_LORA_RANK:Q_LORA_RANK + KV_LORA_RANK]
    o = Q_LORA_RANK + KV_LORA_RANK
    b_gate = z[:, o:o + CONV_WIDTH]
    c_gate = z[:, o + CONV_WIDTH:o + 2 * CONV_WIDTH]
    val = z[:, o + 2 * CONV_WIDTH:o + 3 * CONV_WIDTH]
    k_pe = z[:, o + 3 * CONV_WIDTH:o + 3 * CONV_WIDTH + QK_ROPE_DIM]

    ang_t = invf_ref[...] * pos_ref[0].astype(F32)
    cos_t = jnp.cos(ang_t)
    sin_t = jnp.sin(ang_t)
    cos4 = jnp.concatenate([cos_t] * N_HEADS, axis=0).T
    sin4 = jnp.concatenate([sin_t] * N_HEADS, axis=0).T

    q = jnp.dot(_rms(c_q, gq_ref[...]).astype(BF16), wuq_ref[...], preferred_element_type=F32)
    qn = N_HEADS * QK_NOPE_DIM
    x1 = q[:, qn:qn + LANES]
    x2 = q[:, qn + LANES:qn + 2 * LANES]
    o1 = x1 * cos4 - x2 * sin4
    o2 = x1 * sin4 + x2 * cos4
    scale = QK_DIM ** -0.5

    kv = jnp.dot(_rms(c_kv, gkv_ref[...]).astype(BF16), wukv_ref[...], preferred_element_type=F32)
    k1 = k_pe[:, :ROPE_HALF]
    k2 = k_pe[:, ROPE_HALF:]
    c1 = cos4[:, :ROPE_HALF]
    s1 = sin4[:, :ROPE_HALF]
    kr = jnp.concatenate([k1 * c1 - k2 * s1, k1 * s1 + k2 * c1], axis=1)

    for h in range(N_HEADS):
        qh = jnp.concatenate(
            [q[:, h * QK_NOPE_DIM:(h + 1) * QK_NOPE_DIM],
             o1[:, h * ROPE_HALF:(h + 1) * ROPE_HALF],
             o2[:, h * ROPE_HALF:(h + 1) * ROPE_HALF]], axis=1) * scale
        q_ref[0, h] = qh.astype(BF16)
        base = h * (QK_NOPE_DIM + V_HEAD_DIM)
        kh = jnp.concatenate([kv[:, base:base + QK_NOPE_DIM], kr], axis=1)
        k_ref[0, h] = kh.astype(BF16)
        vh = kv[:, base + QK_NOPE_DIM:base + QK_NOPE_DIM + V_HEAD_DIM]
        v_ref[0, h] = jnp.concatenate([vh, jnp.ones_like(vh)], axis=1).astype(BF16)

    u = c_gate * val

    @pl.when(s_idx == 0)
    def _():
        ubuf_ref[0:SUBLANES, :] = jnp.zeros((SUBLANES, CONV_WIDTH), F32)

    ubuf_ref[SUBLANES:SUBLANES + tm, :] = u
    u_m1 = ubuf_ref[SUBLANES - 1:SUBLANES - 1 + tm, :]
    u_m2 = ubuf_ref[SUBLANES - 2:SUBLANES - 2 + tm, :]
    cw = convw_ref[...]
    y = cw[0:1, :] * u_m2 + cw[1:2, :] * u_m1 + cw[2:3, :] * u
    conv_ref[0] = _rms(b_gate * y, gconv_ref[...]).astype(BF16)
    ubuf_ref[0:SUBLANES, :] = ubuf_ref[tm:tm + SUBLANES, :]


def _inproj(x, positions, invf4, mix_norm, w_in_r, q_norm, w_uq_r, kv_norm, w_ukv, conv_w, conv_out_norm):
    B, S, D = x.shape
    tm = TM_IN
    const = lambda b, s: (0, 0)
    return pl.pallas_call(
        _inproj_kernel,
        grid=(B, S // tm),
        in_specs=[
            pl.BlockSpec((1, tm, D), lambda b, s: (b, s, 0)),
            pl.BlockSpec((1, 1, tm), lambda b, s: (b, 0, s)),
            pl.BlockSpec(invf4.shape, const),
            pl.BlockSpec(mix_norm.shape, const),
            pl.BlockSpec(w_in_r.shape, const),
            pl.BlockSpec(q_norm.shape, const),
            pl.BlockSpec(w_uq_r.shape, const),
            pl.BlockSpec(kv_norm.shape, const),
            pl.BlockSpec(w_ukv.shape, const),
            pl.BlockSpec(conv_w.shape, const),
            pl.BlockSpec(conv_out_norm.shape, const),
        ],
        out_specs=[
            pl.BlockSpec((1, N_HEADS, tm, QK_DIM), lambda b, s: (b, 0, s, 0)),
            pl.BlockSpec((1, N_HEADS, tm, QK_DIM), lambda b, s: (b, 0, s, 0)),
            pl.BlockSpec((1, N_HEADS, tm, V_EXT), lambda b, s: (b, 0, s, 0)),
            pl.BlockSpec((1, tm, CONV_WIDTH), lambda b, s: (b, s, 0)),
        ],
        out_shape=[
            jax.ShapeDtypeStruct((B, N_HEADS, S, QK_DIM), BF16),
            jax.ShapeDtypeStruct((B, N_HEADS, S, QK_DIM), BF16),
            jax.ShapeDtypeStruct((B, N_HEADS, S, V_EXT), BF16),
            jax.ShapeDtypeStruct((B, S, CONV_WIDTH), BF16),
        ],
        scratch_shapes=[pltpu.VMEM((tm + 2 * SUBLANES, CONV_WIDTH), F32)],
        compiler_params=pltpu.CompilerParams(
            dimension_semantics=("arbitrary", "arbitrary"), vmem_limit_bytes=VMEM_LIMIT),
        name="inproj",
    )(x, positions, invf4, mix_norm, w_in_r, q_norm, w_uq_r, kv_norm, w_ukv, conv_w, conv_out_norm)


def _attn_kernel(q_ref, k_ref, v_ref, o_ref):
    tq = q_ref.shape[2]
    i = pl.program_id(1)
    neg = jnp.float32(-1e30)

    rq = tq // ATT_ROW_SPLIT
    chains = [(h, r) for h in range(N_HEADS) for r in range(ATT_ROW_SPLIT)]

    def step(j, carry, masked):
        start = pl.multiple_of(j * TK, TK)
        new = []
        for c, (h, r) in enumerate(chains):
            m, acc = carry[c]
            kj = k_ref[0, h, pl.ds(start, TK), :]
            vj = v_ref[0, h, pl.ds(start, TK), :]
            s = lax.dot_general(q_ref[0, h, r * rq:(r + 1) * rq, :], kj, (((1,), (1,)), ((), ())),
                                preferred_element_type=F32)
            if masked:
                row = lax.broadcasted_iota(jnp.int32, (rq, TK), 0) + r * rq
                col = lax.broadcasted_iota(jnp.int32, (rq, TK), 1)
                s = jnp.where(col <= row, s, neg)
            m_new = jnp.maximum(m, jnp.max(s, axis=-1, keepdims=True))
            alpha = jnp.exp(m - m_new)
            p = jnp.exp(s - m_new)
            acc = alpha * acc + jnp.dot(p.astype(BF16), vj, preferred_element_type=F32)
            new.append((m_new, acc))
        return tuple(new)

    init = tuple((jnp.full((rq, 1), neg, F32), jnp.zeros((rq, V_EXT), F32)) for _ in chains)
    carry = lax.fori_loop(0, i, lambda j, c: step(j, c, False), init)
    final = step(i, carry, True)
    for c, (h, r) in enumerate(chains):
        acc = final[c][1]
        o_ref[0, r * rq:(r + 1) * rq, h * V_HEAD_DIM:(h + 1) * V_HEAD_DIM] = (
            acc[:, :V_HEAD_DIM] / acc[:, V_HEAD_DIM:V_HEAD_DIM + 1]).astype(BF16)


def _attention(q, k, v):
    B, H, S, _ = q.shape
    assert TQ == TK
    return pl.pallas_call(
        _attn_kernel,
        grid=(B, S // TQ),
        in_specs=[
            pl.BlockSpec((1, H, TQ, QK_DIM), lambda b, i: (b, 0, i, 0)),
            pl.BlockSpec((1, H, S, QK_DIM), lambda b, i: (b, 0, 0, 0)),
            pl.BlockSpec((1, H, S, V_EXT), lambda b, i: (b, 0, 0, 0)),
        ],
        out_specs=pl.BlockSpec((1, TQ, H * V_HEAD_DIM), lambda b, i: (b, i, 0)),
        out_shape=jax.ShapeDtypeStruct((B, S, H * V_HEAD_DIM), BF16),
        compiler_params=pltpu.CompilerParams(
            dimension_semantics=("arbitrary", "arbitrary"), vmem_limit_bytes=VMEM_LIMIT),
        name="attention",
    )(q, k, v)


def _outproj_kernel(x_ref, attn_ref, conv_ref, ga_ref, wo_ref, gmoe_ref, wr_hi_ref, wr_lo_ref, br_ref,
                    h1_ref, hn_ref, route_ref, gate_ref, counts_ref, carry_ref):
    tm = x_ref.shape[0]
    step = pl.program_id(0)

    @pl.when(step == 0)
    def _():
        carry_ref[...] = jnp.zeros_like(carry_ref)

    attn_n = _rms(attn_ref[...].astype(F32), ga_ref[...]).astype(BF16)
    mixed = jnp.concatenate([attn_n, conv_ref[...]], axis=1)
    h1 = x_ref[...] + jnp.dot(mixed, wo_ref[...], preferred_element_type=F32)
    h1_ref[...] = h1

    hn = _rms(h1, gmoe_ref[...])
    for j in range(ROW_CHUNKS):
        hn_ref[pl.ds(j, tm, stride=SUBLANES), :] = hn[:, j * LANES:(j + 1) * LANES]

    hn_hi = hn.astype(BF16)
    hn_lo = (hn - hn_hi.astype(F32)).astype(BF16)
    logits = (jnp.dot(hn_hi, wr_hi_ref[...], preferred_element_type=F32)
              + jnp.dot(hn_hi, wr_lo_ref[...], preferred_element_type=F32)
              + jnp.dot(hn_lo, wr_hi_ref[...], preferred_element_type=F32)) + br_ref[...]

    lane = lax.broadcasted_iota(jnp.int32, (tm, N_EXPERTS), 1).astype(F32)
    work = logits
    vals, idxs, hots = [], [], []
    for _ in range(TOP_K):
        mx = jnp.max(work, axis=-1, keepdims=True)
        idx = jnp.min(jnp.where(work == mx, lane, float(N_EXPERTS)), axis=-1, keepdims=True)
        hot = lane == idx
        work = jnp.where(hot, -jnp.inf, work)
        vals.append(mx)
        idxs.append(idx)
        hots.append(hot)

    exps = [jnp.exp(v - vals[0]) for v in vals]
    denom = exps[0] + exps[1] + exps[2] + exps[3]
    gates = [e / denom for e in exps]

    sel = (hots[0] | hots[1] | hots[2] | hots[3]).astype(F32)
    r = lax.broadcasted_iota(jnp.int32, (tm, tm), 0)
    c = lax.broadcasted_iota(jnp.int32, (tm, tm), 1)
    tri = (c < r).astype(BF16)
    rank_mat = jnp.dot(tri, sel.astype(BF16), preferred_element_type=F32) + carry_ref[...]
    carry_ref[...] = carry_ref[...] + jnp.sum(sel, axis=0, keepdims=True)
    counts_ref[...] = carry_ref[...].astype(jnp.int32)

    ranks = [jnp.sum(jnp.where(h, rank_mat, 0.0), axis=-1, keepdims=True) for h in hots]
    lane8 = lax.broadcasted_iota(jnp.int32, (tm, 2 * TOP_K), 1)
    route = jnp.zeros((tm, 2 * TOP_K), F32)
    gate_out = jnp.zeros((tm, TOP_K), F32)
    lane4 = lax.broadcasted_iota(jnp.int32, (tm, TOP_K), 1)
    for k in range(TOP_K):
        route = jnp.where(lane8 == k, idxs[k], route)
        route = jnp.where(lane8 == TOP_K + k, ranks[k], route)
        gate_out = jnp.where(lane4 == k, gates[k], gate_out)
    route_ref[...] = route.astype(jnp.int32)
    gate_ref[...] = gate_out


def _outproj(x2, attn2, conv2, attn_out_norm, w_o, moe_norm, w_router, b_router):
    N, D = x2.shape
    wr_hi = w_router.astype(BF16)
    wr_lo = (w_router - wr_hi.astype(F32)).astype(BF16)
    tm = TM_OUT
    const = lambda i: (0, 0)
    return pl.pallas_call(
        _outproj_kernel,
        grid=(N // tm,),
        in_specs=[
            pl.BlockSpec((tm, D), lambda i: (i, 0)),
            pl.BlockSpec((tm, ATTN_WIDTH), lambda i: (i, 0)),
            pl.BlockSpec((tm, CONV_WIDTH), lambda i: (i, 0)),
            pl.BlockSpec(attn_out_norm.shape, const),
            pl.BlockSpec(w_o.shape, const),
            pl.BlockSpec(moe_norm.shape, const),
            pl.BlockSpec(w_router.shape, const),
            pl.BlockSpec(w_router.shape, const),
            pl.BlockSpec(b_router.shape, const),
        ],
        out_specs=[
            pl.BlockSpec((tm, D), lambda i: (i, 0)),
            pl.BlockSpec((tm * SUBLANES, LANES), lambda i: (i, 0)),
            pl.BlockSpec((tm, 2 * TOP_K), lambda i: (i, 0)),
            pl.BlockSpec((tm, TOP_K), lambda i: (i, 0)),
            pl.BlockSpec((1, N_EXPERTS), const),
        ],
        out_shape=[
            jax.ShapeDtypeStruct((N, D), F32),
            jax.ShapeDtypeStruct((N * SUBLANES, LANES), F32),
            jax.ShapeDtypeStruct((N, 2 * TOP_K), jnp.int32),
            jax.ShapeDtypeStruct((N, TOP_K), F32),
            jax.ShapeDtypeStruct((1, N_EXPERTS), jnp.int32),
        ],
        scratch_shapes=[pltpu.VMEM((1, N_EXPERTS), F32)],
        compiler_params=pltpu.CompilerParams(
            dimension_semantics=("arbitrary",), vmem_limit_bytes=VMEM_LIMIT),
        name="outproj_router",
    )(x2, attn2, conv2, attn_out_norm, w_o, moe_norm, wr_hi, wr_lo, b_router)


def _row_copy(src_ref, src_row, dst_ref, dst_row, sem):
    s = pl.multiple_of(src_row * SUBLANES, SUBLANES)
    d = pl.multiple_of(dst_row * SUBLANES, SUBLANES)
    return pltpu.make_async_copy(src_ref.at[pl.ds(s, SUBLANES), :], dst_ref.at[pl.ds(d, SUBLANES), :], sem)


def _dispatch_kernel(fill_lo_ref, fill_hi_ref, dest_ref, hn_ref, zero_ref, xs_ref, sems, zsem):
    step = pl.program_id(0)

    @pl.when(step == 0)
    def _():
        def per_segment(e, _):
            def fill(r, _):
                _row_copy(zero_ref, 0, xs_ref, r, zsem).start()
                return 0
            lax.fori_loop(fill_lo_ref[e], fill_hi_ref[e], fill, 0)

            def drain(r, _):
                _row_copy(zero_ref, 0, xs_ref, r, zsem).wait()
                return 0
            lax.fori_loop(fill_lo_ref[e], fill_hi_ref[e], drain, 0)
            return 0
        lax.fori_loop(0, N_EXPERTS + 1, per_segment, 0)

    def batch(c, slot, start):
        def body(t, _):
            tok = c * DISPATCH_CHUNK + t
            for k in range(TOP_K):
                cp = _row_copy(hn_ref, tok, xs_ref, dest_ref[TOP_K * tok + k], sems.at[slot])
                if start:
                    cp.start()
                else:
                    cp.wait()
            return 0
        lax.fori_loop(0, DISPATCH_CHUNK, body, 0)

    def pipelined(cc, _):
        even = 2 * cc
        batch(even, 0, start=True)

        @pl.when(cc > 0)
        def _():
            batch(even - 1, 1, start=False)
        batch(even + 1, 1, start=True)
        batch(even, 0, start=False)
        return 0

    n_batches = TD // DISPATCH_CHUNK
    lax.fori_loop(0, n_batches // 2, pipelined, 0)
    batch(n_batches - 1, 1, start=False)


def _dispatch(fill_lo, fill_hi, dest_flat, hn_rows, n_rows_padded):
    N = hn_rows.shape[0] // SUBLANES
    zero_row = jnp.zeros((SUBLANES, LANES), F32)
    grid_spec = pltpu.PrefetchScalarGridSpec(
        num_scalar_prefetch=2,
        grid=(N // TD,),
        in_specs=[
            pl.BlockSpec((TD * TOP_K,), lambda i, *_: (i,), memory_space=pltpu.SMEM),
            pl.BlockSpec((TD * SUBLANES, LANES), lambda i, *_: (i, 0)),
            pl.BlockSpec((SUBLANES, LANES), lambda i, *_: (0, 0)),
        ],
        out_specs=pl.BlockSpec(memory_space=pl.ANY),
        scratch_shapes=[pltpu.SemaphoreType.DMA((2,)), pltpu.SemaphoreType.DMA(())],
    )
    return pl.pallas_call(
        _dispatch_kernel,
        grid_spec=grid_spec,
        out_shape=jax.ShapeDtypeStruct((n_rows_padded * SUBLANES, LANES), F32),
        compiler_params=pltpu.CompilerParams(
            dimension_semantics=("arbitrary",), has_side_effects=True),
        name="dispatch",
    )(fill_lo, fill_hi, dest_flat, hn_rows, zero_row)


def _ffn_kernel(blk_e_ref, nblk_ref, xs_ref, w1_ref, b1_ref, w2_ref, b2_ref, y_ref, w1b_ref, w2b_ref):
    i = pl.program_id(0)
    used = i < nblk_ref[0]

    @pl.when(used & ((i == 0) | (blk_e_ref[i] != blk_e_ref[jnp.maximum(i - 1, 0)])))
    def _():
        for r in range(0, D_MODEL, W_CAST_ROWS):
            w1b_ref[r:r + W_CAST_ROWS, :] = w1_ref[0, r:r + W_CAST_ROWS, :].astype(BF16)
        for r in range(0, D_EXPERT, W_CAST_ROWS):
            w2b_ref[r:r + W_CAST_ROWS, :] = w2_ref[0, r:r + W_CAST_ROWS, :].astype(BF16)

    @pl.when(used)
    def _():
        xb = jnp.concatenate(
            [xs_ref[pl.ds(j, TMF, stride=SUBLANES), :] for j in range(ROW_CHUNKS)], axis=1)
        h = jnp.dot(xb.astype(BF16), w1b_ref[...], preferred_element_type=F32) + b1_ref[0]
        gate = jnp.minimum(h[:, :D_EXPERT], SWIGLU_LIMIT)
        up = jnp.clip(h[:, D_EXPERT:], -SWIGLU_LIMIT, SWIGLU_LIMIT)
        act = gate * jax.nn.sigmoid(SWIGLU_ALPHA * gate) * (up + 1.0)
        y = jnp.dot(act.astype(BF16), w2b_ref[...], preferred_element_type=F32) + b2_ref[0]
        for j in range(ROW_CHUNKS):
            y_ref[pl.ds(j, TMF, stride=SUBLANES), :] = y[:, j * LANES:(j + 1) * LANES]

    @pl.when(i >= nblk_ref[0])
    def _():
        y_ref[...] = jnp.zeros_like(y_ref)


def _ffn(blk_expert, n_used, xs_rows, w1, b1, w2, b2):
    n_blocks = blk_expert.shape[0]

    def row_map(i, be, nb):
        return (i, 0)

    def w_map(i, be, nb):
        return (be[i], 0, 0)

    grid_spec = pltpu.PrefetchScalarGridSpec(
        num_scalar_prefetch=2,
        grid=(n_blocks,),
        in_specs=[
            pl.BlockSpec((TMF * SUBLANES, LANES), row_map),
            pl.BlockSpec((1, D_MODEL, 2 * D_EXPERT), w_map),
            pl.BlockSpec((1, 1, 2 * D_EXPERT), w_map),
            pl.BlockSpec((1, D_EXPERT, D_MODEL), w_map),
            pl.BlockSpec((1, 1, D_MODEL), w_map),
        ],
        out_specs=pl.BlockSpec((TMF * SUBLANES, LANES), row_map),
        scratch_shapes=[pltpu.VMEM((D_MODEL, 2 * D_EXPERT), BF16), pltpu.VMEM((D_EXPERT, D_MODEL), BF16)],
    )
    return pl.pallas_call(
        _ffn_kernel,
        grid_spec=grid_spec,
        out_shape=jax.ShapeDtypeStruct(xs_rows.shape, F32),
        compiler_params=pltpu.CompilerParams(
            dimension_semantics=("arbitrary",), vmem_limit_bytes=VMEM_LIMIT),
        name="expert_ffn",
    )(blk_expert, n_used, xs_rows, w1, b1, w2, b2)


def _final_kernel(dest_ref, dest_next_ref, h1_ref, gate_ref, p_ref, y_ref, gple_ref, wg_ref, wp_ref,
                  gpost_ref, gfin_ref, out_ref, ybuf0_ref, ybuf1_ref, sems):
    step = pl.program_id(0)
    n_steps = pl.num_programs(0)
    bufs = (ybuf0_ref, ybuf1_ref)

    def gather(idx_ref, slot, start):
        def body(t, _):
            for k in range(TOP_K):
                cp = _row_copy(y_ref, idx_ref[TOP_K * t + k], bufs[slot], k * TF + t, sems.at[slot])
                if start:
                    cp.start()
                else:
                    cp.wait()
            return 0
        lax.fori_loop(0, TF, body, 0)

    def compute(slot):
        g = gate_ref[...]
        cols = []
        for j in range(ROW_CHUNKS):
            acc = jnp.zeros((TF, LANES), F32)
            for k in range(TOP_K):
                rows = bufs[slot][pl.ds(k * TF * SUBLANES + j, TF, stride=SUBLANES), :]
                acc = acc + g[:, k:k + 1] * rows
            cols.append(acc)
        h2 = h1_ref[...] + jnp.concatenate(cols, axis=1)
        gate = jax.nn.sigmoid(jnp.dot(_rms(h2, gple_ref[...]).astype(BF16), wg_ref[...],
                                      preferred_element_type=F32))
        ple = _rms(jnp.dot(p_ref[...].astype(BF16), wp_ref[...], preferred_element_type=F32),
                   gpost_ref[...])
        out_ref[...] = _rms(h2 + gate * ple, gfin_ref[...])

    @pl.when(step == 0)
    def _():
        gather(dest_ref, 0, start=True)

    for slot in range(2):
        @pl.when(step % 2 == slot)
        def _(slot=slot):
            @pl.when(step + 1 < n_steps)
            def _():
                gather(dest_next_ref, 1 - slot, start=True)
            gather(dest_ref, slot, start=False)
            compute(slot)


def _final(dest_flat, h1, gates, p2, y_rows, ple_norm, w_ple_gate, w_ple_proj, ple_post_norm, final_norm):
    N, D = h1.shape
    const = lambda i: (0, 0)
    n_steps = N // TF
    return pl.pallas_call(
        _final_kernel,
        grid=(n_steps,),
        in_specs=[
            pl.BlockSpec((TF * TOP_K,), lambda i: (i,), memory_space=pltpu.SMEM),
            pl.BlockSpec((TF * TOP_K,), lambda i: (jnp.minimum(i + 1, n_steps - 1),),
                         memory_space=pltpu.SMEM),
            pl.BlockSpec((TF, D), lambda i: (i, 0)),
            pl.BlockSpec((TF, TOP_K), lambda i: (i, 0)),
            pl.BlockSpec((TF, PLE_DIM), lambda i: (i, 0)),
            pl.BlockSpec(memory_space=pl.ANY),
            pl.BlockSpec(ple_norm.shape, const),
            pl.BlockSpec(w_ple_gate.shape, const),
            pl.BlockSpec(w_ple_proj.shape, const),
            pl.BlockSpec(ple_post_norm.shape, const),
            pl.BlockSpec(final_norm.shape, const),
        ],
        out_specs=pl.BlockSpec((TF, D), lambda i: (i, 0)),
        out_shape=jax.ShapeDtypeStruct((N, D), F32),
        scratch_shapes=[pltpu.VMEM((TOP_K * TF * SUBLANES, LANES), F32),
                        pltpu.VMEM((TOP_K * TF * SUBLANES, LANES), F32),
                        pltpu.SemaphoreType.DMA((2,))],
        compiler_params=pltpu.CompilerParams(
            dimension_semantics=("arbitrary",), vmem_limit_bytes=VMEM_LIMIT),
        name="combine_ple_final",
    )(dest_flat, dest_flat, h1, gates, p2, y_rows, ple_norm, w_ple_gate, w_ple_proj, ple_post_norm,
      final_norm)


def _reorder_w_in(w):
    o = Q_LORA_RANK + KV_LORA_RANK
    return jnp.concatenate([w[:, :o], w[:, o + QK_ROPE_DIM:], w[:, o:o + QK_ROPE_DIM]], axis=1)


def _reorder_w_uq(w):
    w4 = w.reshape(Q_LORA_RANK, N_HEADS, QK_DIM)
    nope = w4[:, :, :QK_NOPE_DIM].reshape(Q_LORA_RANK, N_HEADS * QK_NOPE_DIM)
    r1 = w4[:, :, QK_NOPE_DIM:QK_NOPE_DIM + ROPE_HALF].reshape(Q_LORA_RANK, N_HEADS * ROPE_HALF)
    r2 = w4[:, :, QK_NOPE_DIM + ROPE_HALF:].reshape(Q_LORA_RANK, N_HEADS * ROPE_HALF)
    return jnp.concatenate([nope, r1, r2], axis=1)


def kernel(x, p, positions, mix_norm, w_in, q_norm, w_uq, kv_norm, w_ukv, conv_w, attn_out_norm, conv_out_norm, w_o, moe_norm, w_router, b_router, w1, b1, w2, b2, ple_norm, w_ple_gate, w_ple_proj, ple_post_norm, final_norm):
    B, S, D = x.shape
    N = B * S
    assert p.shape[0] == 1, "single layer"
    half = ROPE_HALF
    inv_freq = ROPE_THETA ** (-jnp.arange(half, dtype=F32) / half)
    invf_col = inv_freq[:, None]

    q, k, v, conv_n = _inproj(
        x, positions[:, None, :], invf_col, mix_norm, _reorder_w_in(w_in[0]).astype(BF16), q_norm,
        _reorder_w_uq(w_uq[0]).astype(BF16), kv_norm, w_ukv[0].astype(BF16), conv_w[0], conv_out_norm)
    attn = _attention(q, k, v)

    h1, hn_rows, route, gates, counts = _outproj(
        x.reshape(N, D), attn.reshape(N, ATTN_WIDTH), conv_n.reshape(N, CONV_WIDTH), attn_out_norm,
        w_o[0].astype(BF16), moe_norm, w_router[0], b_router)

    counts = counts[0]
    padded = ((counts + TMF - 1) // TMF) * TMF
    pad_ends = jnp.cumsum(padded)
    pad_starts = pad_ends - padded
    n_blocks = (N * TOP_K) // TMF + N_EXPERTS
    dest = pad_starts[route[:, :TOP_K]] + route[:, TOP_K:]
    dest_flat = dest.reshape(N * TOP_K).astype(jnp.int32)
    blk_start = jnp.arange(n_blocks, dtype=jnp.int32) * TMF
    blk_expert = jnp.minimum(
        jnp.sum((pad_ends[None, :] <= blk_start[:, None]).astype(jnp.int32), axis=1), N_EXPERTS - 1)
    n_used = (pad_ends[-1:] // TMF).astype(jnp.int32)

    total_rows = jnp.full((1,), n_blocks * TMF, jnp.int32)
    fill_lo = jnp.concatenate([pad_starts + counts, pad_ends[-1:]]).astype(jnp.int32)
    fill_hi = jnp.concatenate([pad_ends, total_rows]).astype(jnp.int32)
    xs_rows = _dispatch(fill_lo, fill_hi, dest_flat, hn_rows, n_blocks * TMF)
    y_rows = _ffn(blk_expert, n_used, xs_rows, w1[0], b1[0][:, None, :], w2[0], b2[0][:, None, :])

    out = _final(dest_flat, h1, gates, p[0].reshape(N, PLE_DIM), y_rows, ple_norm,
                 w_ple_gate[0].astype(BF16), w_ple_proj[0].astype(BF16), ple_post_norm,
                 final_norm.reshape(1, D))
    return out.reshape(B, S, D)
```

```python
import functools

import jax
import jax.numpy as jnp
from jax import lax
from jax.experimental import pallas as pl
from jax.experimental.pallas import tpu as pltpu

D_MODEL = 1024
PLE_DIM = 256
N_HEADS = 4
QK_NOPE_DIM = 128
QK_ROPE_DIM = 64
ROPE_HALF = QK_ROPE_DIM // 2
V_HEAD_DIM = 128
V_EXT = 2 * V_HEAD_DIM
QK_DIM = QK_NOPE_DIM + QK_ROPE_DIM
Q_LORA_RANK = 256
KV_LORA_RANK = 128
ROPE_THETA = 10000.0
ATTN_WIDTH = N_HEADS * V_HEAD_DIM
CONV_WIDTH = 512
CONV_K = 3
N_EXPERTS = 32
TOP_K = 4
D_EXPERT = 1024
SWIGLU_ALPHA = 1.702
SWIGLU_LIMIT = 7.0
EPS = 1e-6

SUBLANES = 8
LANES = 128
ROW_CHUNKS = D_MODEL // LANES

TM_IN = 512
TQ = 512
TK = 512
ATT_ROW_SPLIT = 2
TM_OUT = 512
TD = 512
DISPATCH_CHUNK = 64
TMF = 512
W_CAST_ROWS = 256
TF = 256

VMEM_LIMIT = 56 * 1024 * 1024

BF16 = jnp.bfloat16
F32 = jnp.float32


def _rms(x, g):
    return x * lax.rsqrt(jnp.mean(x * x, axis=-1, keepdims=True) + EPS) * g


def _inproj_kernel(x_ref, pos_ref, invf_ref, gmix_ref, win_ref, gq_ref, wuq_ref, gkv_ref, wukv_ref,
                   convw_ref, gconv_ref, q_ref, k_ref, v_ref, conv_ref, ubuf_ref):
    tm = x_ref.shape[1]
    s_idx = pl.program_id(1)

    xn = _rms(x_ref[0], gmix_ref[...])
    z = jnp.dot(xn.astype(BF16), win_ref[...], preferred_element_type=F32)
    c_q = z[:, 0:Q_LORA_RANK]
    c_kv = z[:, Q_LORA_RANK:Q_LORA_RANK + KV_LORA_RANK]
    o = Q_LORA_RANK + KV_LORA_RANK
    b_gate = z[:, o:o + CONV_WIDTH]
    c_gate = z[:, o + CONV_WIDTH:o + 2 * CONV_WIDTH]
    val = z[:, o + 2 * CONV_WIDTH:o + 3 * CONV_WIDTH]
    k_pe = z[:, o + 3 * CONV_WIDTH:o + 3 * CONV_WIDTH + QK_ROPE_DIM]

    ang_t = invf_ref[...] * pos_ref[0].astype(F32)
    cos_t = jnp.cos(ang_t)
    sin_t = jnp.sin(ang_t)
    cos4 = jnp.concatenate([cos_t] * N_HEADS, axis=0).T
    sin4 = jnp.concatenate([sin_t] * N_HEADS, axis=0).T

    q = jnp.dot(_rms(c_q, gq_ref[...]).astype(BF16), wuq_ref[...], preferred_element_type=F32)
    qn = N_HEADS * QK_NOPE_DIM
    x1 = q[:, qn:qn + LANES]
    x2 = q[:, qn + LANES:qn + 2 * LANES]
    o1 = x1 * cos4 - x2 * sin4
    o2 = x1 * sin4 + x2 * cos4
    scale = QK_DIM ** -0.5

    kv = jnp.dot(_rms(c_kv, gkv_ref[...]).astype(BF16), wukv_ref[...], preferred_element_type=F32)
    k1 = k_pe[:, :ROPE_HALF]
    k2 = k_pe[:, ROPE_HALF:]
    c1 = cos4[:, :ROPE_HALF]
    s1 = sin4[:, :ROPE_HALF]
    kr = jnp.concatenate([k1 * c1 - k2 * s1, k1 * s1 + k2 * c1], axis=1)

    for h in range(N_HEADS):
        qh = jnp.concatenate(
            [q[:, h * QK_NOPE_DIM:(h + 1) * QK_NOPE_DIM],
             o1[:, h * ROPE_HALF:(h + 1) * ROPE_HALF],
             o2[:, h * ROPE_HALF:(h + 1) * ROPE_HALF]], axis=1) * scale
        q_ref[0, h] = qh.astype(BF16)
        base = h * (QK_NOPE_DIM + V_HEAD_DIM)
        kh = jnp.concatenate([kv[:, base:base + QK_NOPE_DIM], kr], axis=1)
        k_ref[0, h] = kh.astype(BF16)
        vh = kv[:, base + QK_NOPE_DIM:base + QK_NOPE_DIM + V_HEAD_DIM]
        v_ref[0, h] = jnp.concatenate([vh, jnp.ones_like(vh)], axis=1).astype(BF16)

    u = c_gate * val

    @pl.when(s_idx == 0)
    def _():
        ubuf_ref[0:SUBLANES, :] = jnp.zeros((SUBLANES, CONV_WIDTH), F32)

    ubuf_ref[SUBLANES:SUBLANES + tm, :] = u
    u_m1 = ubuf_ref[SUBLANES - 1:SUBLANES - 1 + tm, :]
    u_m2 = ubuf_ref[SUBLANES - 2:SUBLANES - 2 + tm, :]
    cw = convw_ref[...]
    y = cw[0:1, :] * u_m2 + cw[1:2, :] * u_m1 + cw[2:3, :] * u
    conv_ref[0] = _rms(b_gate * y, gconv_ref[...]).astype(BF16)
    ubuf_ref[0:SUBLANES, :] = ubuf_ref[tm:tm + SUBLANES, :]


def _inproj(x, positions, invf4, mix_norm, w_in_r, q_norm, w_uq_r, kv_norm, w_ukv, conv_w, conv_out_norm):
    B, S, D = x.shape
    tm = TM_IN
    const = lambda b, s: (0, 0)
    return pl.pallas_call(
        _inproj_kernel,
        grid=(B, S // tm),
        in_specs=[
            pl.BlockSpec((1, tm, D), lambda b, s: (b, s, 0)),
            pl.BlockSpec((1, 1, tm), lambda b, s: (b, 0, s)),
            pl.BlockSpec(invf4.shape, const),
            pl.BlockSpec(mix_norm.shape, const),
            pl.BlockSpec(w_in_r.shape, const),
            pl.BlockSpec(q_norm.shape, const),
            pl.BlockSpec(w_uq_r.shape, const),
            pl.BlockSpec(kv_norm.shape, const),
            pl.BlockSpec(w_ukv.shape, const),
            pl.BlockSpec(conv_w.shape, const),
            pl.BlockSpec(conv_out_norm.shape, const),
        ],
        out_specs=[
            pl.BlockSpec((1, N_HEADS, tm, QK_DIM), lambda b, s: (b, 0, s, 0)),
            pl.BlockSpec((1, N_HEADS, tm, QK_DIM), lambda b, s: (b, 0, s, 0)),
            pl.BlockSpec((1, N_HEADS, tm, V_EXT), lambda b, s: (b, 0, s, 0)),
            pl.BlockSpec((1, tm, CONV_WIDTH), lambda b, s: (b, s, 0)),
        ],
        out_shape=[
            jax.ShapeDtypeStruct((B, N_HEADS, S, QK_DIM), BF16),
            jax.ShapeDtypeStruct((B, N_HEADS, S, QK_DIM), BF16),
            jax.ShapeDtypeStruct((B, N_HEADS, S, V_EXT), BF16),
            jax.ShapeDtypeStruct((B, S, CONV_WIDTH), BF16),
        ],
        scratch_shapes=[pltpu.VMEM((tm + 2 * SUBLANES, CONV_WIDTH), F32)],
        compiler_params=pltpu.CompilerParams(
            dimension_semantics=("arbitrary", "arbitrary"), vmem_limit_bytes=VMEM_LIMIT),
        name="inproj",
    )(x, positions, invf4, mix_norm, w_in_r, q_norm, w_uq_r, kv_norm, w_ukv, conv_w, conv_out_norm)


def _attn_kernel(q_ref, k_ref, v_ref, o_ref):
    tq = q_ref.shape[2]
    i = pl.program_id(1)
    neg = jnp.float32(-1e30)

    rq = tq // ATT_ROW_SPLIT
    chains = [(h, r) for h in range(N_HEADS) for r in range(ATT_ROW_SPLIT)]

    def step(j, carry, masked):
        start = pl.multiple_of(j * TK, TK)
        new = []
        for c, (h, r) in enumerate(chains):
            m, acc = carry[c]
            kj = k_ref[0, h, pl.ds(start, TK), :]
            vj = v_ref[0, h, pl.ds(start, TK), :]
            s = lax.dot_general(q_ref[0, h, r * rq:(r + 1) * rq, :], kj, (((1,), (1,)), ((), ())),
                                preferred_element_type=F32)
            if masked:
                row = lax.broadcasted_iota(jnp.int32, (rq, TK), 0) + r * rq
                col = lax.broadcasted_iota(jnp.int32, (rq, TK), 1)
                s = jnp.where(col <= row, s, neg)
            m_new = jnp.maximum(m, jnp.max(s, axis=-1, keepdims=True))
            alpha = jnp.exp(m - m_new)
            p = jnp.exp(s - m_new)
            acc = alpha * acc + jnp.dot(p.astype(BF16), vj, preferred_element_type=F32)
            new.append((m_new, acc))
        return tuple(new)

    init = tuple((jnp.full((rq, 1), neg, F32), jnp.zeros((rq, V_EXT), F32)) for _ in chains)
    carry = lax.fori_loop(0, i, lambda j, c: step(j, c, False), init)
    final = step(i, carry, True)
    for c, (h, r) in enumerate(chains):
        acc = final[c][1]
        o_ref[0, r * rq:(r + 1) * rq, h * V_HEAD_DIM:(h + 1) * V_HEAD_DIM] = (
            acc[:, :V_HEAD_DIM] / acc[:, V_HEAD_DIM:V_HEAD_DIM + 1]).astype(BF16)


def _attention(q, k, v):
    B, H, S, _ = q.shape
    assert TQ == TK
    return pl.pallas_call(
        _attn_kernel,
        grid=(B, S // TQ),
        in_specs=[
            pl.BlockSpec((1, H, TQ, QK_DIM), lambda b, i: (b, 0, i, 0)),
            pl.BlockSpec((1, H, S, QK_DIM), lambda b, i: (b, 0, 0, 0)),
            pl.BlockSpec((1, H, S, V_EXT), lambda b, i: (b, 0, 0, 0)),
        ],
        out_specs=pl.BlockSpec((1, TQ, H * V_HEAD_DIM), lambda b, i: (b, i, 0)),
        out_shape=jax.ShapeDtypeStruct((B, S, H * V_HEAD_DIM), BF16),
        compiler_params=pltpu.CompilerParams(
            dimension_semantics=("arbitrary", "arbitrary"), vmem_limit_bytes=VMEM_LIMIT),
        name="attention",
    )(q, k, v)


def _outproj_kernel(x_ref, attn_ref, conv_ref, ga_ref, wo_ref, gmoe_ref, wr_hi_ref, wr_lo_ref, br_ref,
                    h1_ref, hn_ref, route_ref, gate_ref, counts_ref, carry_ref):
    tm = x_ref.shape[0]
    step = pl.program_id(0)

    @pl.when(step == 0)
    def _():
        carry_ref[...] = jnp.zeros_like(carry_ref)

    attn_n = _rms(attn_ref[...].astype(F32), ga_ref[...]).astype(BF16)
    mixed = jnp.concatenate([attn_n, conv_ref[...]], axis=1)
    h1 = x_ref[...] + jnp.dot(mixed, wo_ref[...], preferred_element_type=F32)
    h1_ref[...] = h1

    hn = _rms(h1, gmoe_ref[...])
    for j in range(ROW_CHUNKS):
        hn_ref[pl.ds(j, tm, stride=SUBLANES), :] = hn[:, j * LANES:(j + 1) * LANES]

    hn_hi = hn.astype(BF16)
    hn_lo = (hn - hn_hi.astype(F32)).astype(BF16)
    logits = (jnp.dot(hn_hi, wr_hi_ref[...], preferred_element_type=F32)
              + jnp.dot(hn_hi, wr_lo_ref[...], preferred_element_type=F32)
              + jnp.dot(hn_lo, wr_hi_ref[...], preferred_element_type=F32)) + br_ref[...]

    lane = lax.broadcasted_iota(jnp.int32, (tm, N_EXPERTS), 1).astype(F32)
    work = logits
    vals, idxs, hots = [], [], []
    for _ in range(TOP_K):
        mx = jnp.max(work, axis=-1, keepdims=True)
        idx = jnp.min(jnp.where(work == mx, lane, float(N_EXPERTS)), axis=-1, keepdims=True)
        hot = lane == idx
        work = jnp.where(hot, -jnp.inf, work)
        vals.append(mx)
        idxs.append(idx)
        hots.append(hot)

    exps = [jnp.exp(v - vals[0]) for v in vals]
    denom = exps[0] + exps[1] + exps[2] + exps[3]
    gates = [e / denom for e in exps]

    sel = (hots[0] | hots[1] | hots[2] | hots[3]).astype(F32)
    r = lax.broadcasted_iota(jnp.int32, (tm, tm), 0)
    c = lax.broadcasted_iota(jnp.int32, (tm, tm), 1)
    tri = (c < r).astype(BF16)
    rank_mat = jnp.dot(tri, sel.astype(BF16), preferred_element_type=F32) + carry_ref[...]
    carry_ref[...] = carry_ref[...] + jnp.sum(sel, axis=0, keepdims=True)
    counts_ref[...] = carry_ref[...].astype(jnp.int32)

    ranks = [jnp.sum(jnp.where(h, rank_mat, 0.0), axis=-1, keepdims=True) for h in hots]
    lane8 = lax.broadcasted_iota(jnp.int32, (tm, 2 * TOP_K), 1)
    route = jnp.zeros((tm, 2 * TOP_K), F32)
    gate_out = jnp.zeros((tm, TOP_K), F32)
    lane4 = lax.broadcasted_iota(jnp.int32, (tm, TOP_K), 1)
    for k in range(TOP_K):
        route = jnp.where(lane8 == k, idxs[k], route)
        route = jnp.where(lane8 == TOP_K + k, ranks[k], route)
        gate_out = jnp.where(lane4 == k, gates[k], gate_out)
    route_ref[...] = route.astype(jnp.int32)
    gate_ref[...] = gate_out


def _outproj(x2, attn2, conv2, attn_out_norm, w_o, moe_norm, w_router, b_router):
    N, D = x2.shape
    wr_hi = w_router.astype(BF16)
    wr_lo = (w_router - wr_hi.astype(F32)).astype(BF16)
    tm = TM_OUT
    const = lambda i: (0, 0)
    return pl.pallas_call(
        _outproj_kernel,
        grid=(N // tm,),
        in_specs=[
            pl.BlockSpec((tm, D), lambda i: (i, 0)),
            pl.BlockSpec((tm, ATTN_WIDTH), lambda i: (i, 0)),
            pl.BlockSpec((tm, CONV_WIDTH), lambda i: (i, 0)),
            pl.BlockSpec(attn_out_norm.shape, const),
            pl.BlockSpec(w_o.shape, const),
            pl.BlockSpec(moe_norm.shape, const),
            pl.BlockSpec(w_router.shape, const),
            pl.BlockSpec(w_router.shape, const),
            pl.BlockSpec(b_router.shape, const),
        ],
        out_specs=[
            pl.BlockSpec((tm, D), lambda i: (i, 0)),
            pl.BlockSpec((tm * SUBLANES, LANES), lambda i: (i, 0)),
            pl.BlockSpec((tm, 2 * TOP_K), lambda i: (i, 0)),
            pl.BlockSpec((tm, TOP_K), lambda i: (i, 0)),
            pl.BlockSpec((1, N_EXPERTS), const),
        ],
        out_shape=[
            jax.ShapeDtypeStruct((N, D), F32),
            jax.ShapeDtypeStruct((N * SUBLANES, LANES), F32),
            jax.ShapeDtypeStruct((N, 2 * TOP_K), jnp.int32),
            jax.ShapeDtypeStruct((N, TOP_K), F32),
            jax.ShapeDtypeStruct((1, N_EXPERTS), jnp.int32),
        ],
        scratch_shapes=[pltpu.VMEM((1, N_EXPERTS), F32)],
        compiler_params=pltpu.CompilerParams(
            dimension_semantics=("arbitrary",), vmem_limit_bytes=VMEM_LIMIT),
        name="outproj_router",
    )(x2, attn2, conv2, attn_out_norm, w_o, moe_norm, wr_hi, wr_lo, b_router)


def _row_copy(src_ref, src_row, dst_ref, dst_row, sem):
    s = pl.multiple_of(src_row * SUBLANES, SUBLANES)
    d = pl.multiple_of(dst_row * SUBLANES, SUBLANES)
    return pltpu.make_async_copy(src_ref.at[pl.ds(s, SUBLANES), :], dst_ref.at[pl.ds(d, SUBLANES), :], sem)


def _dispatch_kernel(fill_lo_ref, fill_hi_ref, dest_ref, hn_ref, zero_ref, xs_ref, sems, zsem):
    step = pl.program_id(0)

    @pl.when(step == 0)
    def _():
        def per_segment(e, _):
            def fill(r, _):
                _row_copy(zero_ref, 0, xs_ref, r, zsem).start()
                return 0
            lax.fori_loop(fill_lo_ref[e], fill_hi_ref[e], fill, 0)

            def drain(r, _):
                _row_copy(zero_ref, 0, xs_ref, r, zsem).wait()
                return 0
            lax.fori_loop(fill_lo_ref[e], fill_hi_ref[e], drain, 0)
            return 0
        lax.fori_loop(0, N_EXPERTS + 1, per_segment, 0)

    def batch(c, slot, start):
        def body(t, _):
            tok = c * DISPATCH_CHUNK + t
            for k in range(TOP_K):
                cp = _row_copy(hn_ref, tok, xs_ref, dest_ref[TOP_K * tok + k], sems.at[slot])
                if start:
                    cp.start(priority=k % 2)
                else:
                    cp.wait()
            return 0
        lax.fori_loop(0, DISPATCH_CHUNK, body, 0)

    def pipelined(cc, _):
        even = 2 * cc
        batch(even, 0, start=True)

        @pl.when(cc > 0)
        def _():
            batch(even - 1, 1, start=False)
        batch(even + 1, 1, start=True)
        batch(even, 0, start=False)
        return 0

    n_batches = TD // DISPATCH_CHUNK
    lax.fori_loop(0, n_batches // 2, pipelined, 0)
    batch(n_batches - 1, 1, start=False)


def _dispatch(fill_lo, fill_hi, dest_flat, hn_rows, n_rows_padded):
    N = hn_rows.shape[0] // SUBLANES
    zero_row = jnp.zeros((SUBLANES, LANES), F32)
    grid_spec = pltpu.PrefetchScalarGridSpec(
        num_scalar_prefetch=2,
        grid=(N // TD,),
        in_specs=[
            pl.BlockSpec((TD * TOP_K,), lambda i, *_: (i,), memory_space=pltpu.SMEM),
            pl.BlockSpec((TD * SUBLANES, LANES), lambda i, *_: (i, 0)),
            pl.BlockSpec((SUBLANES, LANES), lambda i, *_: (0, 0)),
        ],
        out_specs=pl.BlockSpec(memory_space=pl.ANY),
        scratch_shapes=[pltpu.SemaphoreType.DMA((2,)), pltpu.SemaphoreType.DMA(())],
    )
    return pl.pallas_call(
        _dispatch_kernel,
        grid_spec=grid_spec,
        out_shape=jax.ShapeDtypeStruct((n_rows_padded * SUBLANES, LANES), F32),
        compiler_params=pltpu.CompilerParams(
            dimension_semantics=("arbitrary",), has_side_effects=True),
        name="dispatch",
    )(fill_lo, fill_hi, dest_flat, hn_rows, zero_row)


def _ffn_kernel(blk_e_ref, nblk_ref, xs_ref, w1_ref, b1_ref, w2_ref, b2_ref, y_ref, w1b_ref, w2b_ref):
    i = pl.program_id(0)
    used = i < nblk_ref[0]

    @pl.when(used & ((i == 0) | (blk_e_ref[i] != blk_e_ref[jnp.maximum(i - 1, 0)])))
    def _():
        for r in range(0, D_MODEL, W_CAST_ROWS):
            w1b_ref[r:r + W_CAST_ROWS, :] = w1_ref[0, r:r + W_CAST_ROWS, :].astype(BF16)
        for r in range(0, D_EXPERT, W_CAST_ROWS):
            w2b_ref[r:r + W_CAST_ROWS, :] = w2_ref[0, r:r + W_CAST_ROWS, :].astype(BF16)

    @pl.when(used)
    def _():
        xb = jnp.concatenate(
            [xs_ref[pl.ds(j, TMF, stride=SUBLANES), :] for j in range(ROW_CHUNKS)], axis=1)
        h = jnp.dot(xb.astype(BF16), w1b_ref[...], preferred_element_type=F32) + b1_ref[0]
        gate = jnp.minimum(h[:, :D_EXPERT], SWIGLU_LIMIT)
        up = jnp.clip(h[:, D_EXPERT:], -SWIGLU_LIMIT, SWIGLU_LIMIT)
        act = gate * jax.nn.sigmoid(SWIGLU_ALPHA * gate) * (up + 1.0)
        y = jnp.dot(act.astype(BF16), w2b_ref[...], preferred_element_type=F32) + b2_ref[0]
        for j in range(ROW_CHUNKS):
            y_ref[pl.ds(j, TMF, stride=SUBLANES), :] = y[:, j * LANES:(j + 1) * LANES]

    @pl.when(i >= nblk_ref[0])
    def _():
        y_ref[...] = jnp.zeros_like(y_ref)


def _ffn(blk_expert, n_used, xs_rows, w1, b1, w2, b2):
    n_blocks = blk_expert.shape[0]

    def row_map(i, be, nb):
        return (i, 0)

    def w_map(i, be, nb):
        return (be[i], 0, 0)

    grid_spec = pltpu.PrefetchScalarGridSpec(
        num_scalar_prefetch=2,
        grid=(n_blocks,),
        in_specs=[
            pl.BlockSpec((TMF * SUBLANES, LANES), row_map),
            pl.BlockSpec((1, D_MODEL, 2 * D_EXPERT), w_map),
            pl.BlockSpec((1, 1, 2 * D_EXPERT), w_map),
            pl.BlockSpec((1, D_EXPERT, D_MODEL), w_map),
            pl.BlockSpec((1, 1, D_MODEL), w_map),
        ],
        out_specs=pl.BlockSpec((TMF * SUBLANES, LANES), row_map),
        scratch_shapes=[pltpu.VMEM((D_MODEL, 2 * D_EXPERT), BF16), pltpu.VMEM((D_EXPERT, D_MODEL), BF16)],
    )
    return pl.pallas_call(
        _ffn_kernel,
        grid_spec=grid_spec,
        out_shape=jax.ShapeDtypeStruct(xs_rows.shape, F32),
        compiler_params=pltpu.CompilerParams(
            dimension_semantics=("arbitrary",), vmem_limit_bytes=VMEM_LIMIT),
        name="expert_ffn",
    )(blk_expert, n_used, xs_rows, w1, b1, w2, b2)


def _final_kernel(dest_ref, dest_next_ref, h1_ref, gate_ref, p_ref, y_ref, gple_ref, wg_ref, wp_ref,
                  gpost_ref, gfin_ref, out_ref, ybuf0_ref, ybuf1_ref, sems):
    step = pl.program_id(0)
    n_steps = pl.num_programs(0)
    bufs = (ybuf0_ref, ybuf1_ref)

    def gather(idx_ref, slot, start):
        def body(t, _):
            for k in range(TOP_K):
                cp = _row_copy(y_ref, idx_ref[TOP_K * t + k], bufs[slot], k * TF + t, sems.at[slot])
                if start:
                    cp.start(priority=k % 2)
                else:
                    cp.wait()
            return 0
        lax.fori_loop(0, TF, body, 0)

    def compute(slot):
        g = gate_ref[...]
        gk = [jnp.broadcast_to(g[:, k:k + 1], (TF, LANES)) for k in range(TOP_K)]
        cols = []
        for j in range(ROW_CHUNKS):
            acc = gk[0] * bufs[slot][pl.ds(j, TF, stride=SUBLANES), :]
            for k in range(1, TOP_K):
                acc = acc + gk[k] * bufs[slot][pl.ds(k * TF * SUBLANES + j, TF, stride=SUBLANES), :]
            cols.append(acc)
        h2 = h1_ref[...] + jnp.concatenate(cols, axis=1)
        gate = jax.nn.sigmoid(jnp.dot(_rms(h2, gple_ref[...]).astype(BF16), wg_ref[...],
                                      preferred_element_type=F32))
        ple = _rms(jnp.dot(p_ref[...].astype(BF16), wp_ref[...], preferred_element_type=F32),
                   gpost_ref[...])
        out_ref[...] = _rms(h2 + gate * ple, gfin_ref[...])

    @pl.when(step == 0)
    def _():
        gather(dest_ref, 0, start=True)

    for slot in range(2):
        @pl.when(step % 2 == slot)
        def _(slot=slot):
            @pl.when(step + 1 < n_steps)
            def _():
                gather(dest_next_ref, 1 - slot, start=True)
            gather(dest_ref, slot, start=False)
            compute(slot)


def _final(dest_flat, h1, gates, p2, y_rows, ple_norm, w_ple_gate, w_ple_proj, ple_post_norm, final_norm):
    N, D = h1.shape
    const = lambda i: (0, 0)
    n_steps = N // TF
    return pl.pallas_call(
        _final_kernel,
        grid=(n_steps,),
        in_specs=[
            pl.BlockSpec((TF * TOP_K,), lambda i: (i,), memory_space=pltpu.SMEM),
            pl.BlockSpec((TF * TOP_K,), lambda i: (jnp.minimum(i + 1, n_steps - 1),),
                         memory_space=pltpu.SMEM),
            pl.BlockSpec((TF, D), lambda i: (i, 0)),
            pl.BlockSpec((TF, TOP_K), lambda i: (i, 0)),
            pl.BlockSpec((TF, PLE_DIM), lambda i: (i, 0)),
            pl.BlockSpec(memory_space=pl.ANY),
            pl.BlockSpec(ple_norm.shape, const),
            pl.BlockSpec(w_ple_gate.shape, const),
            pl.BlockSpec(w_ple_proj.shape, const),
            pl.BlockSpec(ple_post_norm.shape, const),
            pl.BlockSpec(final_norm.shape, const),
        ],
        out_specs=pl.BlockSpec((TF, D), lambda i: (i, 0)),
        out_shape=jax.ShapeDtypeStruct((N, D), F32),
        scratch_shapes=[pltpu.VMEM((TOP_K * TF * SUBLANES, LANES), F32),
                        pltpu.VMEM((TOP_K * TF * SUBLANES, LANES), F32),
                        pltpu.SemaphoreType.DMA((2,))],
        compiler_params=pltpu.CompilerParams(
            dimension_semantics=("arbitrary",), vmem_limit_bytes=VMEM_LIMIT),
        name="combine_ple_final",
    )(dest_flat, dest_flat, h1, gates, p2, y_rows, ple_norm, w_ple_gate, w_ple_proj, ple_post_norm,
      final_norm)


def _reorder_w_in(w):
    o = Q_LORA_RANK + KV_LORA_RANK
    return jnp.concatenate([w[:, :o], w[:, o + QK_ROPE_DIM:], w[:, o:o + QK_ROPE_DIM]], axis=1)


def _reorder_w_uq(w):
    w4 = w.reshape(Q_LORA_RANK, N_HEADS, QK_DIM)
    nope = w4[:, :, :QK_NOPE_DIM].reshape(Q_LORA_RANK, N_HEADS * QK_NOPE_DIM)
    r1 = w4[:, :, QK_NOPE_DIM:QK_NOPE_DIM + ROPE_HALF].reshape(Q_LORA_RANK, N_HEADS * ROPE_HALF)
    r2 = w4[:, :, QK_NOPE_DIM + ROPE_HALF:].reshape(Q_LORA_RANK, N_HEADS * ROPE_HALF)
    return jnp.concatenate([nope, r1, r2], axis=1)


def kernel(x, p, positions, mix_norm, w_in, q_norm, w_uq, kv_norm, w_ukv, conv_w, attn_out_norm, conv_out_norm, w_o, moe_norm, w_router, b_router, w1, b1, w2, b2, ple_norm, w_ple_gate, w_ple_proj, ple_post_norm, final_norm):
    B, S, D = x.shape
    N = B * S
    assert p.shape[0] == 1, "single layer"
    half = ROPE_HALF
    inv_freq = ROPE_THETA ** (-jnp.arange(half, dtype=F32) / half)
    invf_col = inv_freq[:, None]

    q, k, v, conv_n = _inproj(
        x, positions[:, None, :], invf_col, mix_norm, _reorder_w_in(w_in[0]).astype(BF16), q_norm,
        _reorder_w_uq(w_uq[0]).astype(BF16), kv_norm, w_ukv[0].astype(BF16), conv_w[0], conv_out_norm)
    attn = _attention(q, k, v)

    h1, hn_rows, route, gates, counts = _outproj(
        x.reshape(N, D), attn.reshape(N, ATTN_WIDTH), conv_n.reshape(N, CONV_WIDTH), attn_out_norm,
        w_o[0].astype(BF16), moe_norm, w_router[0], b_router)

    counts = counts[0]
    padded = ((counts + TMF - 1) // TMF) * TMF
    pad_ends = jnp.cumsum(padded)
    pad_starts = pad_ends - padded
    n_blocks = (N * TOP_K) // TMF + N_EXPERTS
    dest = pad_starts[route[:, :TOP_K]] + route[:, TOP_K:]
    dest_flat = dest.reshape(N * TOP_K).astype(jnp.int32)
    blk_start = jnp.arange(n_blocks, dtype=jnp.int32) * TMF
    blk_expert = jnp.minimum(
        jnp.sum((pad_ends[None, :] <= blk_start[:, None]).astype(jnp.int32), axis=1), N_EXPERTS - 1)
    n_used = (pad_ends[-1:] // TMF).astype(jnp.int32)

    total_rows = jnp.full((1,), n_blocks * TMF, jnp.int32)
    fill_lo = jnp.concatenate([pad_starts + counts, pad_ends[-1:]]).astype(jnp.int32)
    fill_hi = jnp.concatenate([pad_ends, total_rows]).astype(jnp.int32)
    xs_rows = _dispatch(fill_lo, fill_hi, dest_flat, hn_rows, n_blocks * TMF)
    y_rows = _ffn(blk_expert, n_used, xs_rows, w1[0], b1[0][:, None, :], w2[0], b2[0][:, None, :])

    out = _final(dest_flat, h1, gates, p[0].reshape(N, PLE_DIM), y_rows, ple_norm,
                 w_ple_gate[0].astype(BF16), w_ple_proj[0].astype(BF16), ple_post_norm,
                 final_norm.reshape(1, D))
    return out.reshape(B, S, D)
```

```python
import functools

import jax
import jax.numpy as jnp
from jax import lax
from jax.experimental import pallas as pl
from jax.experimental.pallas import tpu as pltpu

D_MODEL = 1024
PLE_DIM = 256
N_HEADS = 4
QK_NOPE_DIM = 128
QK_ROPE_DIM = 64
ROPE_HALF = QK_ROPE_DIM // 2
V_HEAD_DIM = 128
V_EXT = 2 * V_HEAD_DIM
QK_DIM = QK_NOPE_DIM + QK_ROPE_DIM
Q_LORA_RANK = 256
KV_LORA_RANK = 128
ROPE_THETA = 10000.0
ATTN_WIDTH = N_HEADS * V_HEAD_DIM
CONV_WIDTH = 512
CONV_K = 3
N_EXPERTS = 32
TOP_K = 4
D_EXPERT = 1024
SWIGLU_ALPHA = 1.702
SWIGLU_LIMIT = 7.0
EPS = 1e-6

SUBLANES = 8
LANES = 128
ROW_CHUNKS = D_MODEL // LANES

TM_IN = 512
TQ = 512
TK = 512
ATT_ROW_SPLIT = 2
TM_OUT = 512
TD = 512
DISPATCH_CHUNK = 64
ZERO_ROWS = 64
TMF = 512
W_CAST_ROWS = 256
TF = 256

VMEM_LIMIT = 56 * 1024 * 1024

BF16 = jnp.bfloat16
F32 = jnp.float32


def _rms(x, g):
    return x * lax.rsqrt(jnp.mean(x * x, axis=-1, keepdims=True) + EPS) * g


def _inproj_kernel(x_ref, pos_ref, invf_ref, gmix_ref, win_ref, gq_ref, wuq_ref, gkv_ref, wukv_ref,
                   convw_ref, gconv_ref, q_ref, k_ref, v_ref, conv_ref, ubuf_ref):
    tm = x_ref.shape[1]
    s_idx = pl.program_id(1)

    xn = _rms(x_ref[0], gmix_ref[...])
    z = jnp.dot(xn.astype(BF16), win_ref[...], preferred_element_type=F32)
    c_q = z[:, 0:Q_LORA_RANK]
    c_kv = z[:, Q_LORA_RANK:Q_LORA_RANK + KV_LORA_RANK]
    o = Q_LORA_RANK + KV_LORA_RANK
    b_gate = z[:, o:o + CONV_WIDTH]
    c_gate = z[:, o + CONV_WIDTH:o + 2 * CONV_WIDTH]
    val = z[:, o + 2 * CONV_WIDTH:o + 3 * CONV_WIDTH]
    k_pe = z[:, o + 3 * CONV_WIDTH:o + 3 * CONV_WIDTH + QK_ROPE_DIM]

    ang_t = invf_ref[...] * pos_ref[0].astype(F32)
    cos_t = jnp.cos(ang_t)
    sin_t = jnp.sin(ang_t)
    cos4 = jnp.concatenate([cos_t] * N_HEADS, axis=0).T
    sin4 = jnp.concatenate([sin_t] * N_HEADS, axis=0).T

    q = jnp.dot(_rms(c_q, gq_ref[...]).astype(BF16), wuq_ref[...], preferred_element_type=F32)
    qn = N_HEADS * QK_NOPE_DIM
    x1 = q[:, qn:qn + LANES]
    x2 = q[:, qn + LANES:qn + 2 * LANES]
    o1 = x1 * cos4 - x2 * sin4
    o2 = x1 * sin4 + x2 * cos4
    scale = QK_DIM ** -0.5

    kv = jnp.dot(_rms(c_kv, gkv_ref[...]).astype(BF16), wukv_ref[...], preferred_element_type=F32)
    k1 = k_pe[:, :ROPE_HALF]
    k2 = k_pe[:, ROPE_HALF:]
    c1 = cos4[:, :ROPE_HALF]
    s1 = sin4[:, :ROPE_HALF]
    kr = jnp.concatenate([k1 * c1 - k2 * s1, k1 * s1 + k2 * c1], axis=1)

    for h in range(N_HEADS):
        qh = jnp.concatenate(
            [q[:, h * QK_NOPE_DIM:(h + 1) * QK_NOPE_DIM],
             o1[:, h * ROPE_HALF:(h + 1) * ROPE_HALF],
             o2[:, h * ROPE_HALF:(h + 1) * ROPE_HALF]], axis=1) * scale
        q_ref[0, h] = qh.astype(BF16)
        base = h * (QK_NOPE_DIM + V_HEAD_DIM)
        kh = jnp.concatenate([kv[:, base:base + QK_NOPE_DIM], kr], axis=1)
        k_ref[0, h] = kh.astype(BF16)
        vh = kv[:, base + QK_NOPE_DIM:base + QK_NOPE_DIM + V_HEAD_DIM]
        v_ref[0, h] = jnp.concatenate([vh, jnp.ones_like(vh)], axis=1).astype(BF16)

    u = c_gate * val

    @pl.when(s_idx == 0)
    def _():
        ubuf_ref[0:SUBLANES, :] = jnp.zeros((SUBLANES, CONV_WIDTH), F32)

    ubuf_ref[SUBLANES:SUBLANES + tm, :] = u
    u_m1 = ubuf_ref[SUBLANES - 1:SUBLANES - 1 + tm, :]
    u_m2 = ubuf_ref[SUBLANES - 2:SUBLANES - 2 + tm, :]
    cw = convw_ref[...]
    y = cw[0:1, :] * u_m2 + cw[1:2, :] * u_m1 + cw[2:3, :] * u
    conv_ref[0] = _rms(b_gate * y, gconv_ref[...]).astype(BF16)
    ubuf_ref[0:SUBLANES, :] = ubuf_ref[tm:tm + SUBLANES, :]


def _inproj(x, positions, invf4, mix_norm, w_in_r, q_norm, w_uq_r, kv_norm, w_ukv, conv_w, conv_out_norm):
    B, S, D = x.shape
    tm = TM_IN
    const = lambda b, s: (0, 0)
    return pl.pallas_call(
        _inproj_kernel,
        grid=(B, S // tm),
        in_specs=[
            pl.BlockSpec((1, tm, D), lambda b, s: (b, s, 0)),
            pl.BlockSpec((1, 1, tm), lambda b, s: (b, 0, s)),
            pl.BlockSpec(invf4.shape, const),
            pl.BlockSpec(mix_norm.shape, const),
            pl.BlockSpec(w_in_r.shape, const),
            pl.BlockSpec(q_norm.shape, const),
            pl.BlockSpec(w_uq_r.shape, const),
            pl.BlockSpec(kv_norm.shape, const),
            pl.BlockSpec(w_ukv.shape, const),
            pl.BlockSpec(conv_w.shape, const),
            pl.BlockSpec(conv_out_norm.shape, const),
        ],
        out_specs=[
            pl.BlockSpec((1, N_HEADS, tm, QK_DIM), lambda b, s: (b, 0, s, 0)),
            pl.BlockSpec((1, N_HEADS, tm, QK_DIM), lambda b, s: (b, 0, s, 0)),
            pl.BlockSpec((1, N_HEADS, tm, V_EXT), lambda b, s: (b, 0, s, 0)),
            pl.BlockSpec((1, tm, CONV_WIDTH), lambda b, s: (b, s, 0)),
        ],
        out_shape=[
            jax.ShapeDtypeStruct((B, N_HEADS, S, QK_DIM), BF16),
            jax.ShapeDtypeStruct((B, N_HEADS, S, QK_DIM), BF16),
            jax.ShapeDtypeStruct((B, N_HEADS, S, V_EXT), BF16),
            jax.ShapeDtypeStruct((B, S, CONV_WIDTH), BF16),
        ],
        scratch_shapes=[pltpu.VMEM((tm + 2 * SUBLANES, CONV_WIDTH), F32)],
        compiler_params=pltpu.CompilerParams(
            dimension_semantics=("arbitrary", "arbitrary"), vmem_limit_bytes=VMEM_LIMIT),
        name="inproj",
    )(x, positions, invf4, mix_norm, w_in_r, q_norm, w_uq_r, kv_norm, w_ukv, conv_w, conv_out_norm)


def _attn_kernel(q_ref, k_ref, v_ref, o_ref):
    tq = q_ref.shape[2]
    i = pl.program_id(1)
    neg = jnp.float32(-1e30)

    rq = tq // ATT_ROW_SPLIT
    chains = [(h, r) for h in range(N_HEADS) for r in range(ATT_ROW_SPLIT)]

    def step(j, carry, col_off=None):
        start = pl.multiple_of(j * TK, TK)
        new = []
        for c, (h, r) in enumerate(chains):
            m, acc = carry[c]
            if col_off is not None and col_off >= (r + 1) * rq:
                new.append((m, acc))
                continue
            kj = k_ref[0, h, pl.ds(start, TK), :]
            vj = v_ref[0, h, pl.ds(start, TK), :]
            s = lax.dot_general(q_ref[0, h, r * rq:(r + 1) * rq, :], kj, (((1,), (1,)), ((), ())),
                                preferred_element_type=F32)
            if col_off is not None and col_off + TK - 1 > r * rq:
                row = lax.broadcasted_iota(jnp.int32, (rq, TK), 0) + r * rq
                col = lax.broadcasted_iota(jnp.int32, (rq, TK), 1) + col_off
                s = jnp.where(col <= row, s, neg)
            m_new = jnp.maximum(m, jnp.max(s, axis=-1, keepdims=True))
            alpha = jnp.exp(m - m_new)
            p = jnp.exp(s - m_new)
            acc = alpha * acc + jnp.dot(p.astype(BF16), vj, preferred_element_type=F32)
            new.append((m_new, acc))
        return tuple(new)

    init = tuple((jnp.full((rq, 1), neg, F32), jnp.zeros((rq, V_EXT), F32)) for _ in chains)
    ratio = tq // TK
    final = lax.fori_loop(0, i * ratio, lambda j, c: step(j, c), init)
    for d in range(ratio):
        final = step(i * ratio + d, final, col_off=d * TK)
    for c, (h, r) in enumerate(chains):
        acc = final[c][1]
        o_ref[0, r * rq:(r + 1) * rq, h * V_HEAD_DIM:(h + 1) * V_HEAD_DIM] = (
            acc[:, :V_HEAD_DIM] / acc[:, V_HEAD_DIM:V_HEAD_DIM + 1]).astype(BF16)


def _attention(q, k, v):
    B, H, S, _ = q.shape
    assert TQ % TK == 0 and TQ % ATT_ROW_SPLIT == 0
    return pl.pallas_call(
        _attn_kernel,
        grid=(B, S // TQ),
        in_specs=[
            pl.BlockSpec((1, H, TQ, QK_DIM), lambda b, i: (b, 0, i, 0)),
            pl.BlockSpec((1, H, S, QK_DIM), lambda b, i: (b, 0, 0, 0)),
            pl.BlockSpec((1, H, S, V_EXT), lambda b, i: (b, 0, 0, 0)),
        ],
        out_specs=pl.BlockSpec((1, TQ, H * V_HEAD_DIM), lambda b, i: (b, i, 0)),
        out_shape=jax.ShapeDtypeStruct((B, S, H * V_HEAD_DIM), BF16),
        compiler_params=pltpu.CompilerParams(
            dimension_semantics=("arbitrary", "arbitrary"), vmem_limit_bytes=VMEM_LIMIT),
        name="attention",
    )(q, k, v)


def _outproj_kernel(x_ref, attn_ref, conv_ref, ga_ref, wo_ref, gmoe_ref, wr_hi_ref, wr_lo_ref, br_ref,
                    h1_ref, hn_ref, route_ref, gate_ref, counts_ref, carry_ref):
    tm = x_ref.shape[0]
    step = pl.program_id(0)

    @pl.when(step == 0)
    def _():
        carry_ref[...] = jnp.zeros_like(carry_ref)

    attn_n = _rms(attn_ref[...].astype(F32), ga_ref[...]).astype(BF16)
    mixed = jnp.concatenate([attn_n, conv_ref[...]], axis=1)
    h1 = x_ref[...] + jnp.dot(mixed, wo_ref[...], preferred_element_type=F32)
    h1_ref[...] = h1

    hn = _rms(h1, gmoe_ref[...])
    for j in range(ROW_CHUNKS):
        hn_ref[pl.ds(j, tm, stride=SUBLANES), :] = hn[:, j * LANES:(j + 1) * LANES]

    hn_hi = hn.astype(BF16)
    hn_lo = (hn - hn_hi.astype(F32)).astype(BF16)
    logits = (jnp.dot(hn_hi, wr_hi_ref[...], preferred_element_type=F32)
              + jnp.dot(hn_hi, wr_lo_ref[...], preferred_element_type=F32)
              + jnp.dot(hn_lo, wr_hi_ref[...], preferred_element_type=F32)) + br_ref[...]

    lane = lax.broadcasted_iota(jnp.int32, (tm, N_EXPERTS), 1).astype(F32)
    work = logits
    vals, idxs, hots = [], [], []
    for _ in range(TOP_K):
        mx = jnp.max(work, axis=-1, keepdims=True)
        idx = jnp.min(jnp.where(work == mx, lane, float(N_EXPERTS)), axis=-1, keepdims=True)
        hot = lane == idx
        work = jnp.where(hot, -jnp.inf, work)
        vals.append(mx)
        idxs.append(idx)
        hots.append(hot)

    exps = [jnp.exp(v - vals[0]) for v in vals]
    denom = exps[0] + exps[1] + exps[2] + exps[3]
    gates = [e / denom for e in exps]

    sel = (hots[0] | hots[1] | hots[2] | hots[3]).astype(F32)
    r = lax.broadcasted_iota(jnp.int32, (tm, tm), 0)
    c = lax.broadcasted_iota(jnp.int32, (tm, tm), 1)
    tri = (c < r).astype(BF16)
    rank_mat = jnp.dot(tri, sel.astype(BF16), preferred_element_type=F32) + carry_ref[...]
    carry_ref[...] = carry_ref[...] + jnp.sum(sel, axis=0, keepdims=True)
    counts_ref[...] = carry_ref[...].astype(jnp.int32)

    ranks = [jnp.sum(jnp.where(h, rank_mat, 0.0), axis=-1, keepdims=True) for h in hots]
    lane8 = lax.broadcasted_iota(jnp.int32, (tm, 2 * TOP_K), 1)
    route = jnp.zeros((tm, 2 * TOP_K), F32)
    gate_out = jnp.zeros((tm, TOP_K), F32)
    lane4 = lax.broadcasted_iota(jnp.int32, (tm, TOP_K), 1)
    for k in range(TOP_K):
        route = jnp.where(lane8 == k, idxs[k], route)
        route = jnp.where(lane8 == TOP_K + k, ranks[k], route)
        gate_out = jnp.where(lane4 == k, gates[k], gate_out)
    route_ref[...] = route.astype(jnp.int32)
    gate_ref[...] = gate_out


def _outproj(x2, attn2, conv2, attn_out_norm, w_o, moe_norm, w_router, b_router):
    N, D = x2.shape
    wr_hi = w_router.astype(BF16)
    wr_lo = (w_router - wr_hi.astype(F32)).astype(BF16)
    tm = TM_OUT
    const = lambda i: (0, 0)
    return pl.pallas_call(
        _outproj_kernel,
        grid=(N // tm,),
        in_specs=[
            pl.BlockSpec((tm, D), lambda i: (i, 0)),
            pl.BlockSpec((tm, ATTN_WIDTH), lambda i: (i, 0)),
            pl.BlockSpec((tm, CONV_WIDTH), lambda i: (i, 0)),
            pl.BlockSpec(attn_out_norm.shape, const),
            pl.BlockSpec(w_o.shape, const),
            pl.BlockSpec(moe_norm.shape, const),
            pl.BlockSpec(w_router.shape, const),
            pl.BlockSpec(w_router.shape, const),
            pl.BlockSpec(b_router.shape, const),
        ],
        out_specs=[
            pl.BlockSpec((tm, D), lambda i: (i, 0)),
            pl.BlockSpec((tm * SUBLANES, LANES), lambda i: (i, 0)),
            pl.BlockSpec((tm, 2 * TOP_K), lambda i: (i, 0)),
            pl.BlockSpec((tm, TOP_K), lambda i: (i, 0)),
            pl.BlockSpec((1, N_EXPERTS), const),
        ],
        out_shape=[
            jax.ShapeDtypeStruct((N, D), F32),
            jax.ShapeDtypeStruct((N * SUBLANES, LANES), F32),
            jax.ShapeDtypeStruct((N, 2 * TOP_K), jnp.int32),
            jax.ShapeDtypeStruct((N, TOP_K), F32),
            jax.ShapeDtypeStruct((1, N_EXPERTS), jnp.int32),
        ],
        scratch_shapes=[pltpu.VMEM((1, N_EXPERTS), F32)],
        compiler_params=pltpu.CompilerParams(
            dimension_semantics=("arbitrary",), vmem_limit_bytes=VMEM_LIMIT),
        name="outproj_router",
    )(x2, attn2, conv2, attn_out_norm, w_o, moe_norm, wr_hi, wr_lo, b_router)


def _row_copy(src_ref, src_row, dst_ref, dst_row, sem):
    s = pl.multiple_of(src_row * SUBLANES, SUBLANES)
    d = pl.multiple_of(dst_row * SUBLANES, SUBLANES)
    return pltpu.make_async_copy(src_ref.at[pl.ds(s, SUBLANES), :], dst_ref.at[pl.ds(d, SUBLANES), :], sem)


def _dispatch_kernel(fill_lo_ref, fill_hi_ref, dest_ref, hn_ref, zero_ref, xs_ref, sems, zsem):
    step = pl.program_id(0)

    def zero_copy(first_row, n_rows):
        d = pl.multiple_of(first_row * SUBLANES, SUBLANES)
        return pltpu.make_async_copy(zero_ref.at[pl.ds(0, n_rows * SUBLANES), :],
                                     xs_ref.at[pl.ds(d, n_rows * SUBLANES), :], zsem)

    def fill_segments(start):
        def act(cp):
            if start:
                cp.start()
            else:
                cp.wait()

        def per_segment(e, _):
            lo = fill_lo_ref[e]
            n = fill_hi_ref[e] - lo
            n_full = n // ZERO_ROWS

            def full(c, _):
                act(zero_copy(lo + c * ZERO_ROWS, ZERO_ROWS))
                return 0
            lax.fori_loop(0, n_full, full, 0)
            off = lo + n_full * ZERO_ROWS
            rem = n - n_full * ZERO_ROWS
            size = ZERO_ROWS // 2
            while size >= 1:
                @pl.when((rem & size) != 0)
                def _(size=size):
                    act(zero_copy(off + (rem & ~(2 * size - 1)), size))
                size //= 2
            return 0
        lax.fori_loop(0, N_EXPERTS + 1, per_segment, 0)

    @pl.when(step == 0)
    def _():
        fill_segments(start=True)
        fill_segments(start=False)

    def batch(c, slot, start):
        def body(t, _):
            tok = c * DISPATCH_CHUNK + t
            for k in range(TOP_K):
                cp = _row_copy(hn_ref, tok, xs_ref, dest_ref[TOP_K * tok + k], sems.at[slot])
                if start:
                    cp.start(priority=k % 2)
                else:
                    cp.wait()
            return 0
        lax.fori_loop(0, DISPATCH_CHUNK, body, 0)

    def pipelined(cc, _):
        even = 2 * cc
        batch(even, 0, start=True)

        @pl.when(cc > 0)
        def _():
            batch(even - 1, 1, start=False)
        batch(even + 1, 1, start=True)
        batch(even, 0, start=False)
        return 0

    n_batches = TD // DISPATCH_CHUNK
    lax.fori_loop(0, n_batches // 2, pipelined, 0)
    batch(n_batches - 1, 1, start=False)


def _dispatch(fill_lo, fill_hi, dest_flat, hn_rows, n_rows_padded):
    N = hn_rows.shape[0] // SUBLANES
    zero_row = jnp.zeros((ZERO_ROWS * SUBLANES, LANES), F32)
    grid_spec = pltpu.PrefetchScalarGridSpec(
        num_scalar_prefetch=2,
        grid=(N // TD,),
        in_specs=[
            pl.BlockSpec((TD * TOP_K,), lambda i, *_: (i,), memory_space=pltpu.SMEM),
            pl.BlockSpec((TD * SUBLANES, LANES), lambda i, *_: (i, 0)),
            pl.BlockSpec((ZERO_ROWS * SUBLANES, LANES), lambda i, *_: (0, 0)),
        ],
        out_specs=pl.BlockSpec(memory_space=pl.ANY),
        scratch_shapes=[pltpu.SemaphoreType.DMA((2,)), pltpu.SemaphoreType.DMA(())],
    )
    return pl.pallas_call(
        _dispatch_kernel,
        grid_spec=grid_spec,
        out_shape=jax.ShapeDtypeStruct((n_rows_padded * SUBLANES, LANES), F32),
        compiler_params=pltpu.CompilerParams(
            dimension_semantics=("arbitrary",), has_side_effects=True),
        name="dispatch",
    )(fill_lo, fill_hi, dest_flat, hn_rows, zero_row)


def _ffn_kernel(blk_e_ref, nblk_ref, xs_ref, w1_ref, b1_ref, w2_ref, b2_ref, y_ref, w1b_ref, w2b_ref):
    i = pl.program_id(0)
    used = i < nblk_ref[0]

    @pl.when(used & ((i == 0) | (blk_e_ref[i] != blk_e_ref[jnp.maximum(i - 1, 0)])))
    def _():
        for r in range(0, D_MODEL, W_CAST_ROWS):
            w1b_ref[r:r + W_CAST_ROWS, :] = w1_ref[0, r:r + W_CAST_ROWS, :].astype(BF16)
        for r in range(0, D_EXPERT, W_CAST_ROWS):
            w2b_ref[r:r + W_CAST_ROWS, :] = w2_ref[0, r:r + W_CAST_ROWS, :].astype(BF16)

    @pl.when(used)
    def _():
        xb = jnp.concatenate(
            [xs_ref[pl.ds(j, TMF, stride=SUBLANES), :] for j in range(ROW_CHUNKS)], axis=1)
        h = jnp.dot(xb.astype(BF16), w1b_ref[...], preferred_element_type=F32) + b1_ref[0]
        gate = jnp.minimum(h[:, :D_EXPERT], SWIGLU_LIMIT)
        up = jnp.clip(h[:, D_EXPERT:], -SWIGLU_LIMIT, SWIGLU_LIMIT)
        act = gate * jax.nn.sigmoid(SWIGLU_ALPHA * gate) * (up + 1.0)
        y = jnp.dot(act.astype(BF16), w2b_ref[...], preferred_element_type=F32) + b2_ref[0]
        for j in range(ROW_CHUNKS):
            y_ref[pl.ds(j, TMF, stride=SUBLANES), :] = y[:, j * LANES:(j + 1) * LANES]

    @pl.when(i >= nblk_ref[0])
    def _():
        y_ref[...] = jnp.zeros_like(y_ref)


def _ffn(blk_expert, n_used, xs_rows, w1, b1, w2, b2):
    n_blocks = blk_expert.shape[0]

    def row_map(i, be, nb):
        return (i, 0)

    def w_map(i, be, nb):
        return (be[i], 0, 0)

    grid_spec = pltpu.PrefetchScalarGridSpec(
        num_scalar_prefetch=2,
        grid=(n_blocks,),
        in_specs=[
            pl.BlockSpec((TMF * SUBLANES, LANES), row_map),
            pl.BlockSpec((1, D_MODEL, 2 * D_EXPERT), w_map),
            pl.BlockSpec((1, 1, 2 * D_EXPERT), w_map),
            pl.BlockSpec((1, D_EXPERT, D_MODEL), w_map),
            pl.BlockSpec((1, 1, D_MODEL), w_map),
        ],
        out_specs=pl.BlockSpec((TMF * SUBLANES, LANES), row_map),
        scratch_shapes=[pltpu.VMEM((D_MODEL, 2 * D_EXPERT), BF16), pltpu.VMEM((D_EXPERT, D_MODEL), BF16)],
    )
    return pl.pallas_call(
        _ffn_kernel,
        grid_spec=grid_spec,
        out_shape=jax.ShapeDtypeStruct(xs_rows.shape, F32),
        compiler_params=pltpu.CompilerParams(
            dimension_semantics=("arbitrary",), vmem_limit_bytes=VMEM_LIMIT),
        name="expert_ffn",
    )(blk_expert, n_used, xs_rows, w1, b1, w2, b2)


def _final_kernel(dest_ref, dest_next_ref, h1_ref, gate_ref, p_ref, y_ref, gple_ref, wg_ref, wp_ref,
                  gpost_ref, gfin_ref, out_ref, ybuf0_ref, ybuf1_ref, sems):
    step = pl.program_id(0)
    n_steps = pl.num_programs(0)
    bufs = (ybuf0_ref, ybuf1_ref)

    def gather(idx_ref, slot, start):
        def body(t, _):
            for k in range(TOP_K):
                cp = _row_copy(y_ref, idx_ref[TOP_K * t + k], bufs[slot], k * TF + t, sems.at[slot])
                if start:
                    cp.start(priority=k % 2)
                else:
                    cp.wait()
            return 0
        lax.fori_loop(0, TF, body, 0)

    def compute(slot):
        g = gate_ref[...]
        gk = [jnp.broadcast_to(g[:, k:k + 1], (TF, LANES)) for k in range(TOP_K)]
        cols = []
        for j in range(ROW_CHUNKS):
            acc = gk[0] * bufs[slot][pl.ds(j, TF, stride=SUBLANES), :]
            for k in range(1, TOP_K):
                acc = acc + gk[k] * bufs[slot][pl.ds(k * TF * SUBLANES + j, TF, stride=SUBLANES), :]
            cols.append(acc)
        h2 = h1_ref[...] + jnp.concatenate(cols, axis=1)
        gate = jax.nn.sigmoid(jnp.dot(_rms(h2, gple_ref[...]).astype(BF16), wg_ref[...],
                                      preferred_element_type=F32))
        ple = _rms(jnp.dot(p_ref[...].astype(BF16), wp_ref[...], preferred_element_type=F32),
                   gpost_ref[...])
        out_ref[...] = _rms(h2 + gate * ple, gfin_ref[...])

    @pl.when(step == 0)
    def _():
        gather(dest_ref, 0, start=True)

    for slot in range(2):
        @pl.when(step % 2 == slot)
        def _(slot=slot):
            @pl.when(step + 1 < n_steps)
            def _():
                gather(dest_next_ref, 1 - slot, start=True)
            gather(dest_ref, slot, start=False)
            compute(slot)


def _final(dest_flat, h1, gates, p2, y_rows, ple_norm, w_ple_gate, w_ple_proj, ple_post_norm, final_norm):
    N, D = h1.shape
    const = lambda i: (0, 0)
    n_steps = N // TF
    return pl.pallas_call(
        _final_kernel,
        grid=(n_steps,),
        in_specs=[
            pl.BlockSpec((TF * TOP_K,), lambda i: (i,), memory_space=pltpu.SMEM),
            pl.BlockSpec((TF * TOP_K,), lambda i: (jnp.minimum(i + 1, n_steps - 1),),
                         memory_space=pltpu.SMEM),
            pl.BlockSpec((TF, D), lambda i: (i, 0)),
            pl.BlockSpec((TF, TOP_K), lambda i: (i, 0)),
            pl.BlockSpec((TF, PLE_DIM), lambda i: (i, 0)),
            pl.BlockSpec(memory_space=pl.ANY),
            pl.BlockSpec(ple_norm.shape, const),
            pl.BlockSpec(w_ple_gate.shape, const),
            pl.BlockSpec(w_ple_proj.shape, const),
            pl.BlockSpec(ple_post_norm.shape, const),
            pl.BlockSpec(final_norm.shape, const),
        ],
        out_specs=pl.BlockSpec((TF, D), lambda i: (i, 0)),
        out_shape=jax.ShapeDtypeStruct((N, D), F32),
        scratch_shapes=[pltpu.VMEM((TOP_K * TF * SUBLANES, LANES), F32),
                        pltpu.VMEM((TOP_K * TF * SUBLANES, LANES), F32),
                        pltpu.SemaphoreType.DMA((2,))],
        compiler_params=pltpu.CompilerParams(
            dimension_semantics=("arbitrary",), vmem_limit_bytes=VMEM_LIMIT),
        name="combine_ple_final",
    )(dest_flat, dest_flat, h1, gates, p2, y_rows, ple_norm, w_ple_gate, w_ple_proj, ple_post_norm,
      final_norm)


def _reorder_w_in(w):
    o = Q_LORA_RANK + KV_LORA_RANK
    return jnp.concatenate([w[:, :o], w[:, o + QK_ROPE_DIM:], w[:, o:o + QK_ROPE_DIM]], axis=1)


def _reorder_w_uq(w):
    w4 = w.reshape(Q_LORA_RANK, N_HEADS, QK_DIM)
    nope = w4[:, :, :QK_NOPE_DIM].reshape(Q_LORA_RANK, N_HEADS * QK_NOPE_DIM)
    r1 = w4[:, :, QK_NOPE_DIM:QK_NOPE_DIM + ROPE_HALF].reshape(Q_LORA_RANK, N_HEADS * ROPE_HALF)
    r2 = w4[:, :, QK_NOPE_DIM + ROPE_HALF:].reshape(Q_LORA_RANK, N_HEADS * ROPE_HALF)
    return jnp.concatenate([nope, r1, r2], axis=1)


def kernel(x, p, positions, mix_norm, w_in, q_norm, w_uq, kv_norm, w_ukv, conv_w, attn_out_norm, conv_out_norm, w_o, moe_norm, w_router, b_router, w1, b1, w2, b2, ple_norm, w_ple_gate, w_ple_proj, ple_post_norm, final_norm):
    B, S, D = x.shape
    N = B * S
    assert p.shape[0] == 1, "single layer"
    half = ROPE_HALF
    inv_freq = ROPE_THETA ** (-jnp.arange(half, dtype=F32) / half)
    invf_col = inv_freq[:, None]

    q, k, v, conv_n = _inproj(
        x, positions[:, None, :], invf_col, mix_norm, _reorder_w_in(w_in[0]).astype(BF16), q_norm,
        _reorder_w_uq(w_uq[0]).astype(BF16), kv_norm, w_ukv[0].astype(BF16), conv_w[0], conv_out_norm)
    attn = _attention(q, k, v)

    h1, hn_rows, route, gates, counts = _outproj(
        x.reshape(N, D), attn.reshape(N, ATTN_WIDTH), conv_n.reshape(N, CONV_WIDTH), attn_out_norm,
        w_o[0].astype(BF16), moe_norm, w_router[0], b_router)

    counts = counts[0]
    padded = ((counts + TMF - 1) // TMF) * TMF
    pad_ends = jnp.cumsum(padded)
    pad_starts = pad_ends - padded
    n_blocks = (N * TOP_K) // TMF + N_EXPERTS
    dest = pad_starts[route[:, :TOP_K]] + route[:, TOP_K:]
    dest_flat = dest.reshape(N * TOP_K).astype(jnp.int32)
    blk_start = jnp.arange(n_blocks, dtype=jnp.int32) * TMF
    blk_expert = jnp.minimum(
        jnp.sum((pad_ends[None, :] <= blk_start[:, None]).astype(jnp.int32), axis=1), N_EXPERTS - 1)
    n_used = (pad_ends[-1:] // TMF).astype(jnp.int32)

    total_rows = jnp.full((1,), n_blocks * TMF, jnp.int32)
    fill_lo = jnp.concatenate([pad_starts + counts, pad_ends[-1:]]).astype(jnp.int32)
    fill_hi = jnp.concatenate([pad_ends, total_rows]).astype(jnp.int32)
    xs_rows = _dispatch(fill_lo, fill_hi, dest_flat, hn_rows, n_blocks * TMF)
    y_rows = _ffn(blk_expert, n_used, xs_rows, w1[0], b1[0][:, None, :], w2[0], b2[0][:, None, :])

    out = _final(dest_flat, h1, gates, p[0].reshape(N, PLE_DIM), y_rows, ple_norm,
                 w_ple_gate[0].astype(BF16), w_ple_proj[0].astype(BF16), ple_post_norm,
                 final_norm.reshape(1, D))
    return out.reshape(B, S, D)
```

```python
import functools

import jax
import jax.numpy as jnp
from jax import lax
from jax.experimental import pallas as pl
from jax.experimental.pallas import tpu as pltpu

D_MODEL = 1024
PLE_DIM = 256
N_HEADS = 4
QK_NOPE_DIM = 128
QK_ROPE_DIM = 64
ROPE_HALF = QK_ROPE_DIM // 2
V_HEAD_DIM = 128
V_EXT = 2 * V_HEAD_DIM
QK_DIM = QK_NOPE_DIM + QK_ROPE_DIM
Q_LORA_RANK = 256
KV_LORA_RANK = 128
ROPE_THETA = 10000.0
ATTN_WIDTH = N_HEADS * V_HEAD_DIM
CONV_WIDTH = 512
CONV_K = 3
N_EXPERTS = 32
TOP_K = 4
D_EXPERT = 1024
SWIGLU_ALPHA = 1.702
SWIGLU_LIMIT = 7.0
EPS = 1e-6

SUBLANES = 8
LANES = 128
ROW_CHUNKS = D_MODEL // LANES

TM_IN = 512
TQ = 512
TK = 512
ATT_ROW_SPLIT = 2
TM_OUT = 512
TD = 512
DISPATCH_CHUNK = 64
ZERO_ROWS = 64
TMF = 512
W_CAST_ROWS = 256
TF = 256

VMEM_LIMIT = 56 * 1024 * 1024

BF16 = jnp.bfloat16
F32 = jnp.float32


def _rms(x, g):
    return x * lax.rsqrt(jnp.mean(x * x, axis=-1, keepdims=True) + EPS) * g


def _inproj_kernel(x_ref, pos_ref, invf_ref, gmix_ref, win_ref, gq_ref, wuq_ref, gkv_ref, wukv_ref,
                   convw_ref, gconv_ref, q_ref, k_ref, v_ref, conv_ref, ubuf_ref):
    tm = x_ref.shape[1]
    s_idx = pl.program_id(1)

    xn = _rms(x_ref[0], gmix_ref[...])
    z = jnp.dot(xn.astype(BF16), win_ref[...], preferred_element_type=F32)
    c_q = z[:, 0:Q_LORA_RANK]
    c_kv = z[:, Q_LORA_RANK:Q_LORA_RANK + KV_LORA_RANK]
    o = Q_LORA_RANK + KV_LORA_RANK
    b_gate = z[:, o:o + CONV_WIDTH]
    c_gate = z[:, o + CONV_WIDTH:o + 2 * CONV_WIDTH]
    val = z[:, o + 2 * CONV_WIDTH:o + 3 * CONV_WIDTH]
    k_pe = z[:, o + 3 * CONV_WIDTH:o + 3 * CONV_WIDTH + QK_ROPE_DIM]

    ang_t = invf_ref[...] * pos_ref[0].astype(F32)
    cos_t = jnp.cos(ang_t)
    sin_t = jnp.sin(ang_t)
    cos4 = jnp.concatenate([cos_t] * N_HEADS, axis=0).T
    sin4 = jnp.concatenate([sin_t] * N_HEADS, axis=0).T

    q = jnp.dot(_rms(c_q, gq_ref[...]).astype(BF16), wuq_ref[...], preferred_element_type=F32)
    qn = N_HEADS * QK_NOPE_DIM
    x1 = q[:, qn:qn + LANES]
    x2 = q[:, qn + LANES:qn + 2 * LANES]
    o1 = x1 * cos4 - x2 * sin4
    o2 = x1 * sin4 + x2 * cos4
    scale = QK_DIM ** -0.5

    kv = jnp.dot(_rms(c_kv, gkv_ref[...]).astype(BF16), wukv_ref[...], preferred_element_type=F32)
    k1 = k_pe[:, :ROPE_HALF]
    k2 = k_pe[:, ROPE_HALF:]
    c1 = cos4[:, :ROPE_HALF]
    s1 = sin4[:, :ROPE_HALF]
    kr = jnp.concatenate([k1 * c1 - k2 * s1, k1 * s1 + k2 * c1], axis=1)

    for h in range(N_HEADS):
        qh = jnp.concatenate(
            [q[:, h * QK_NOPE_DIM:(h + 1) * QK_NOPE_DIM],
             o1[:, h * ROPE_HALF:(h + 1) * ROPE_HALF],
             o2[:, h * ROPE_HALF:(h + 1) * ROPE_HALF]], axis=1) * scale
        q_ref[0, h] = qh.astype(BF16)
        base = h * (QK_NOPE_DIM + V_HEAD_DIM)
        kh = jnp.concatenate([kv[:, base:base + QK_NOPE_DIM], kr], axis=1)
        k_ref[0, h] = kh.astype(BF16)
        vh = kv[:, base + QK_NOPE_DIM:base + QK_NOPE_DIM + V_HEAD_DIM]
        v_ref[0, h] = jnp.concatenate([vh, jnp.ones_like(vh)], axis=1).astype(BF16)

    u = c_gate * val

    @pl.when(s_idx == 0)
    def _():
        ubuf_ref[0:SUBLANES, :] = jnp.zeros((SUBLANES, CONV_WIDTH), F32)

    ubuf_ref[SUBLANES:SUBLANES + tm, :] = u
    u_m1 = ubuf_ref[SUBLANES - 1:SUBLANES - 1 + tm, :]
    u_m2 = ubuf_ref[SUBLANES - 2:SUBLANES - 2 + tm, :]
    cw = convw_ref[...]
    y = cw[0:1, :] * u_m2 + cw[1:2, :] * u_m1 + cw[2:3, :] * u
    conv_ref[0] = _rms(b_gate * y, gconv_ref[...]).astype(BF16)
    ubuf_ref[0:SUBLANES, :] = ubuf_ref[tm:tm + SUBLANES, :]


def _inproj(x, positions, invf4, mix_norm, w_in_r, q_norm, w_uq_r, kv_norm, w_ukv, conv_w, conv_out_norm):
    B, S, D = x.shape
    tm = TM_IN
    const = lambda b, s: (0, 0)
    return pl.pallas_call(
        _inproj_kernel,
        grid=(B, S // tm),
        in_specs=[
            pl.BlockSpec((1, tm, D), lambda b, s: (b, s, 0)),
            pl.BlockSpec((1, 1, tm), lambda b, s: (b, 0, s)),
            pl.BlockSpec(invf4.shape, const),
            pl.BlockSpec(mix_norm.shape, const),
            pl.BlockSpec(w_in_r.shape, const),
            pl.BlockSpec(q_norm.shape, const),
            pl.BlockSpec(w_uq_r.shape, const),
            pl.BlockSpec(kv_norm.shape, const),
            pl.BlockSpec(w_ukv.shape, const),
            pl.BlockSpec(conv_w.shape, const),
            pl.BlockSpec(conv_out_norm.shape, const),
        ],
        out_specs=[
            pl.BlockSpec((1, N_HEADS, tm, QK_DIM), lambda b, s: (b, 0, s, 0)),
            pl.BlockSpec((1, N_HEADS, tm, QK_DIM), lambda b, s: (b, 0, s, 0)),
            pl.BlockSpec((1, N_HEADS, tm, V_EXT), lambda b, s: (b, 0, s, 0)),
            pl.BlockSpec((1, tm, CONV_WIDTH), lambda b, s: (b, s, 0)),
        ],
        out_shape=[
            jax.ShapeDtypeStruct((B, N_HEADS, S, QK_DIM), BF16),
            jax.ShapeDtypeStruct((B, N_HEADS, S, QK_DIM), BF16),
            jax.ShapeDtypeStruct((B, N_HEADS, S, V_EXT), BF16),
            jax.ShapeDtypeStruct((B, S, CONV_WIDTH), BF16),
        ],
        scratch_shapes=[pltpu.VMEM((tm + 2 * SUBLANES, CONV_WIDTH), F32)],
        compiler_params=pltpu.CompilerParams(
            dimension_semantics=("arbitrary", "arbitrary"), vmem_limit_bytes=VMEM_LIMIT),
        name="inproj",
    )(x, positions, invf4, mix_norm, w_in_r, q_norm, w_uq_r, kv_norm, w_ukv, conv_w, conv_out_norm)


def _attn_kernel(q_ref, k_ref, v_ref, o_ref):
    tq = q_ref.shape[2]
    i = pl.program_id(1)
    neg = jnp.float32(-1e30)

    rq = tq // ATT_ROW_SPLIT
    chains = [(h, r) for h in range(N_HEADS) for r in range(ATT_ROW_SPLIT)]

    def step(j, carry, col_off=None):
        start = pl.multiple_of(j * TK, TK)
        new = []
        for c, (h, r) in enumerate(chains):
            m, acc = carry[c]
            if col_off is not None and col_off >= (r + 1) * rq:
                new.append((m, acc))
                continue
            kj = k_ref[0, h, pl.ds(start, TK), :]
            vj = v_ref[0, h, pl.ds(start, TK), :]
            s = lax.dot_general(q_ref[0, h, r * rq:(r + 1) * rq, :], kj, (((1,), (1,)), ((), ())),
                                preferred_element_type=F32)
            if col_off is not None and col_off + TK - 1 > r * rq:
                row = lax.broadcasted_iota(jnp.int32, (rq, TK), 0) + r * rq
                col = lax.broadcasted_iota(jnp.int32, (rq, TK), 1) + col_off
                s = jnp.where(col <= row, s, neg)
            m_new = jnp.maximum(m, jnp.max(s, axis=-1, keepdims=True))
            alpha = jnp.exp(m - m_new)
            p = jnp.exp(s - m_new)
            acc = alpha * acc + jnp.dot(p.astype(BF16), vj, preferred_element_type=F32)
            new.append((m_new, acc))
        return tuple(new)

    init = tuple((jnp.full((rq, 1), neg, F32), jnp.zeros((rq, V_EXT), F32)) for _ in chains)
    ratio = tq // TK
    final = lax.fori_loop(0, i * ratio, lambda j, c: step(j, c), init)
    for d in range(ratio):
        final = step(i * ratio + d, final, col_off=d * TK)
    for c, (h, r) in enumerate(chains):
        acc = final[c][1]
        o_ref[0, r * rq:(r + 1) * rq, h * V_HEAD_DIM:(h + 1) * V_HEAD_DIM] = (
            acc[:, :V_HEAD_DIM] / acc[:, V_HEAD_DIM:V_HEAD_DIM + 1]).astype(BF16)


def _attention(q, k, v):
    B, H, S, _ = q.shape
    assert TQ % TK == 0 and TQ % ATT_ROW_SPLIT == 0
    return pl.pallas_call(
        _attn_kernel,
        grid=(B, S // TQ),
        in_specs=[
            pl.BlockSpec((1, H, TQ, QK_DIM), lambda b, i: (b, 0, i, 0)),
            pl.BlockSpec((1, H, S, QK_DIM), lambda b, i: (b, 0, 0, 0)),
            pl.BlockSpec((1, H, S, V_EXT), lambda b, i: (b, 0, 0, 0)),
        ],
        out_specs=pl.BlockSpec((1, TQ, H * V_HEAD_DIM), lambda b, i: (b, i, 0)),
        out_shape=jax.ShapeDtypeStruct((B, S, H * V_HEAD_DIM), BF16),
        compiler_params=pltpu.CompilerParams(
            dimension_semantics=("arbitrary", "arbitrary"), vmem_limit_bytes=VMEM_LIMIT),
        name="attention",
    )(q, k, v)


def _outproj_kernel(x_ref, attn_ref, conv_ref, ga_ref, wo_ref, gmoe_ref, wr_hi_ref, wr_lo_ref, br_ref,
                    h1_ref, hn_ref, route_ref, gate_ref, counts_ref, carry_ref):
    tm = x_ref.shape[0]
    step = pl.program_id(0)

    @pl.when(step == 0)
    def _():
        carry_ref[...] = jnp.zeros_like(carry_ref)

    attn_n = _rms(attn_ref[...].astype(F32), ga_ref[...]).astype(BF16)
    mixed = jnp.concatenate([attn_n, conv_ref[...]], axis=1)
    h1 = x_ref[...] + jnp.dot(mixed, wo_ref[...], preferred_element_type=F32)
    h1_ref[...] = h1

    hn = _rms(h1, gmoe_ref[...])
    for j in range(ROW_CHUNKS):
        hn_ref[pl.ds(j, tm, stride=SUBLANES), :] = hn[:, j * LANES:(j + 1) * LANES]

    hn_hi = hn.astype(BF16)
    hn_lo = (hn - hn_hi.astype(F32)).astype(BF16)
    logits = (jnp.dot(hn_hi, wr_hi_ref[...], preferred_element_type=F32)
              + jnp.dot(hn_hi, wr_lo_ref[...], preferred_element_type=F32)
              + jnp.dot(hn_lo, wr_hi_ref[...], preferred_element_type=F32)) + br_ref[...]

    lane = lax.broadcasted_iota(jnp.int32, (tm, N_EXPERTS), 1).astype(F32)
    work = logits
    vals, idxs, hots = [], [], []
    for _ in range(TOP_K):
        mx = jnp.max(work, axis=-1, keepdims=True)
        idx = jnp.min(jnp.where(work == mx, lane, float(N_EXPERTS)), axis=-1, keepdims=True)
        hot = lane == idx
        work = jnp.where(hot, -jnp.inf, work)
        vals.append(mx)
        idxs.append(idx)
        hots.append(hot)

    exps = [jnp.exp(v - vals[0]) for v in vals]
    denom = exps[0] + exps[1] + exps[2] + exps[3]
    gates = [e / denom for e in exps]

    sel = (hots[0] | hots[1] | hots[2] | hots[3]).astype(F32)
    r = lax.broadcasted_iota(jnp.int32, (tm, tm), 0)
    c = lax.broadcasted_iota(jnp.int32, (tm, tm), 1)
    tri = (c < r).astype(BF16)
    rank_mat = jnp.dot(tri, sel.astype(BF16), preferred_element_type=F32) + carry_ref[...]
    carry_ref[...] = carry_ref[...] + jnp.sum(sel, axis=0, keepdims=True)
    counts_ref[...] = carry_ref[...].astype(jnp.int32)

    ranks = [jnp.sum(jnp.where(h, rank_mat, 0.0), axis=-1, keepdims=True) for h in hots]
    lane8 = lax.broadcasted_iota(jnp.int32, (tm, 2 * TOP_K), 1)
    route = jnp.zeros((tm, 2 * TOP_K), F32)
    gate_out = jnp.zeros((tm, TOP_K), F32)
    lane4 = lax.broadcasted_iota(jnp.int32, (tm, TOP_K), 1)
    for k in range(TOP_K):
        route = jnp.where(lane8 == k, idxs[k], route)
        route = jnp.where(lane8 == TOP_K + k, ranks[k], route)
        gate_out = jnp.where(lane4 == k, gates[k], gate_out)
    route_ref[...] = route.astype(jnp.int32)
    gate_ref[...] = gate_out


def _outproj(x2, attn2, conv2, attn_out_norm, w_o, moe_norm, w_router, b_router):
    N, D = x2.shape
    wr_hi = w_router.astype(BF16)
    wr_lo = (w_router - wr_hi.astype(F32)).astype(BF16)
    tm = TM_OUT
    const = lambda i: (0, 0)
    return pl.pallas_call(
        _outproj_kernel,
        grid=(N // tm,),
        in_specs=[
            pl.BlockSpec((tm, D), lambda i: (i, 0)),
            pl.BlockSpec((tm, ATTN_WIDTH), lambda i: (i, 0)),
            pl.BlockSpec((tm, CONV_WIDTH), lambda i: (i, 0)),
            pl.BlockSpec(attn_out_norm.shape, const),
            pl.BlockSpec(w_o.shape, const),
            pl.BlockSpec(moe_norm.shape, const),
            pl.BlockSpec(w_router.shape, const),
            pl.BlockSpec(w_router.shape, const),
            pl.BlockSpec(b_router.shape, const),
        ],
        out_specs=[
            pl.BlockSpec((tm, D), lambda i: (i, 0)),
            pl.BlockSpec((tm * SUBLANES, LANES), lambda i: (i, 0)),
            pl.BlockSpec((tm, 2 * TOP_K), lambda i: (i, 0)),
            pl.BlockSpec((tm, TOP_K), lambda i: (i, 0)),
            pl.BlockSpec((1, N_EXPERTS), const),
        ],
        out_shape=[
            jax.ShapeDtypeStruct((N, D), F32),
            jax.ShapeDtypeStruct((N * SUBLANES, LANES), F32),
            jax.ShapeDtypeStruct((N, 2 * TOP_K), jnp.int32),
            jax.ShapeDtypeStruct((N, TOP_K), F32),
            jax.ShapeDtypeStruct((1, N_EXPERTS), jnp.int32),
        ],
        scratch_shapes=[pltpu.VMEM((1, N_EXPERTS), F32)],
        compiler_params=pltpu.CompilerParams(
            dimension_semantics=("arbitrary",), vmem_limit_bytes=VMEM_LIMIT),
        name="outproj_router",
    )(x2, attn2, conv2, attn_out_norm, w_o, moe_norm, wr_hi, wr_lo, b_router)


def _row_copy(src_ref, src_row, dst_ref, dst_row, sem):
    s = pl.multiple_of(src_row * SUBLANES, SUBLANES)
    d = pl.multiple_of(dst_row * SUBLANES, SUBLANES)
    return pltpu.make_async_copy(src_ref.at[pl.ds(s, SUBLANES), :], dst_ref.at[pl.ds(d, SUBLANES), :], sem)


def _dispatch_kernel(fill_lo_ref, fill_hi_ref, dest_ref, hn_ref, zero_ref, xs_ref, sems, zsem):
    step = pl.program_id(0)

    def zero_copy(first_row, n_rows):
        d = pl.multiple_of(first_row * SUBLANES, SUBLANES)
        return pltpu.make_async_copy(zero_ref.at[pl.ds(0, n_rows * SUBLANES), :],
                                     xs_ref.at[pl.ds(d, n_rows * SUBLANES), :], zsem)

    def fill_segments(start):
        def act(cp):
            if start:
                cp.start()
            else:
                cp.wait()

        def per_segment(e, _):
            lo = fill_lo_ref[e]
            n = fill_hi_ref[e] - lo
            n_full = n // ZERO_ROWS

            def full(c, _):
                act(zero_copy(lo + c * ZERO_ROWS, ZERO_ROWS))
                return 0
            lax.fori_loop(0, n_full, full, 0)
            off = lo + n_full * ZERO_ROWS
            rem = n - n_full * ZERO_ROWS
            size = ZERO_ROWS // 2
            while size >= 1:
                @pl.when((rem & size) != 0)
                def _(size=size):
                    act(zero_copy(off + (rem & ~(2 * size - 1)), size))
                size //= 2
            return 0
        lax.fori_loop(0, N_EXPERTS + 1, per_segment, 0)

    @pl.when(step == 0)
    def _():
        fill_segments(start=True)
        fill_segments(start=False)

    def batch(c, slot, start):
        def body(t, _):
            tok = c * DISPATCH_CHUNK + t
            for k in range(TOP_K):
                cp = _row_copy(hn_ref, tok, xs_ref, dest_ref[TOP_K * tok + k], sems.at[slot])
                if start:
                    cp.start(priority=k % 2)
                else:
                    cp.wait()
            return 0
        lax.fori_loop(0, DISPATCH_CHUNK, body, 0)

    def pipelined(cc, _):
        even = 2 * cc
        batch(even, 0, start=True)

        @pl.when(cc > 0)
        def _():
            batch(even - 1, 1, start=False)
        batch(even + 1, 1, start=True)
        batch(even, 0, start=False)
        return 0

    n_batches = TD // DISPATCH_CHUNK
    lax.fori_loop(0, n_batches // 2, pipelined, 0)
    batch(n_batches - 1, 1, start=False)


def _dispatch(fill_lo, fill_hi, dest_flat, hn_rows, n_rows_padded):
    N = hn_rows.shape[0] // SUBLANES
    zero_row = jnp.zeros((ZERO_ROWS * SUBLANES, LANES), F32)
    grid_spec = pltpu.PrefetchScalarGridSpec(
        num_scalar_prefetch=2,
        grid=(N // TD,),
        in_specs=[
            pl.BlockSpec((TD * TOP_K,), lambda i, *_: (i,), memory_space=pltpu.SMEM),
            pl.BlockSpec((TD * SUBLANES, LANES), lambda i, *_: (i, 0)),
            pl.BlockSpec((ZERO_ROWS * SUBLANES, LANES), lambda i, *_: (0, 0)),
        ],
        out_specs=pl.BlockSpec(memory_space=pl.ANY),
        scratch_shapes=[pltpu.SemaphoreType.DMA((2,)), pltpu.SemaphoreType.DMA(())],
    )
    return pl.pallas_call(
        _dispatch_kernel,
        grid_spec=grid_spec,
        out_shape=jax.ShapeDtypeStruct((n_rows_padded * SUBLANES, LANES), F32),
        compiler_params=pltpu.CompilerParams(
            dimension_semantics=("arbitrary",), has_side_effects=True),
        name="dispatch",
    )(fill_lo, fill_hi, dest_flat, hn_rows, zero_row)


def _ffn_kernel(blk_e_ref, nblk_ref, xs_ref, w1_ref, b1_ref, w2_ref, b2_ref, y_ref, w1b_ref, w2b_ref):
    i = pl.program_id(0)
    used = i < nblk_ref[0]

    @pl.when(used & ((i == 0) | (blk_e_ref[i] != blk_e_ref[jnp.maximum(i - 1, 0)])))
    def _():
        for r in range(0, D_MODEL, W_CAST_ROWS):
            w1b_ref[r:r + W_CAST_ROWS, :] = w1_ref[0, r:r + W_CAST_ROWS, :].astype(BF16)
        for r in range(0, D_EXPERT, W_CAST_ROWS):
            w2b_ref[r:r + W_CAST_ROWS, :] = w2_ref[0, r:r + W_CAST_ROWS, :].astype(BF16)

    @pl.when(used)
    def _():
        xb = jnp.concatenate(
            [xs_ref[pl.ds(j, TMF, stride=SUBLANES), :] for j in range(ROW_CHUNKS)], axis=1)
        h = jnp.dot(xb.astype(BF16), w1b_ref[...], preferred_element_type=F32) + b1_ref[0]
        gate = jnp.minimum(h[:, :D_EXPERT], SWIGLU_LIMIT)
        up = jnp.clip(h[:, D_EXPERT:], -SWIGLU_LIMIT, SWIGLU_LIMIT)
        act = gate * jax.nn.sigmoid(SWIGLU_ALPHA * gate) * (up + 1.0)
        y = jnp.dot(act.astype(BF16), w2b_ref[...], preferred_element_type=F32) + b2_ref[0]
        for j in range(ROW_CHUNKS):
            y_ref[pl.ds(j, TMF, stride=SUBLANES), :] = y[:, j * LANES:(j + 1) * LANES]

    @pl.when(i >= nblk_ref[0])
    def _():
        y_ref[...] = jnp.zeros_like(y_ref)


def _ffn(blk_expert, n_used, xs_rows, w1, b1, w2, b2):
    n_blocks = blk_expert.shape[0]

    def row_map(i, be, nb):
        return (i, 0)

    def w_map(i, be, nb):
        return (be[i], 0, 0)

    grid_spec = pltpu.PrefetchScalarGridSpec(
        num_scalar_prefetch=2,
        grid=(n_blocks,),
        in_specs=[
            pl.BlockSpec((TMF * SUBLANES, LANES), row_map),
            pl.BlockSpec((1, D_MODEL, 2 * D_EXPERT), w_map),
            pl.BlockSpec((1, 1, 2 * D_EXPERT), w_map),
            pl.BlockSpec((1, D_EXPERT, D_MODEL), w_map),
            pl.BlockSpec((1, 1, D_MODEL), w_map),
        ],
        out_specs=pl.BlockSpec((TMF * SUBLANES, LANES), row_map),
        scratch_shapes=[pltpu.VMEM((D_MODEL, 2 * D_EXPERT), BF16), pltpu.VMEM((D_EXPERT, D_MODEL), BF16)],
    )
    return pl.pallas_call(
        _ffn_kernel,
        grid_spec=grid_spec,
        out_shape=jax.ShapeDtypeStruct(xs_rows.shape, F32),
        compiler_params=pltpu.CompilerParams(
            dimension_semantics=("arbitrary",), vmem_limit_bytes=VMEM_LIMIT),
        name="expert_ffn",
    )(blk_expert, n_used, xs_rows, w1, b1, w2, b2)


def _final_kernel(dest_ref, dest_next_ref, h1_ref, gate_ref, p_ref, y_ref, gple_ref, wg_ref, wp_ref,
                  gpost_ref, gfin_ref, out_ref, ybuf0_ref, ybuf1_ref, sems):
    step = pl.program_id(0)
    n_steps = pl.num_programs(0)
    bufs = (ybuf0_ref, ybuf1_ref)

    def gather(idx_ref, slot, start):
        def body(t, _):
            for k in range(TOP_K):
                cp = _row_copy(y_ref, idx_ref[TOP_K * t + k], bufs[slot], k * TF + t, sems.at[slot])
                if start:
                    cp.start(priority=k % 2)
                else:
                    cp.wait()
            return 0
        lax.fori_loop(0, TF, body, 0)

    tokens_per_piece = TF // (ROW_CHUNKS * TOP_K)

    def start_piece(piece, slot):
        for t in range(piece * tokens_per_piece, (piece + 1) * tokens_per_piece):
            for k in range(TOP_K):
                _row_copy(y_ref, dest_next_ref[TOP_K * t + k], bufs[slot], k * TF + t,
                          sems.at[slot]).start(priority=k % 2)

    def compute(slot):
        g = gate_ref[...]
        gk = [jnp.broadcast_to(g[:, k:k + 1], (TF, LANES)) for k in range(TOP_K)]
        cols = []
        for j in range(ROW_CHUNKS):
            acc = gk[0] * bufs[slot][pl.ds(j, TF, stride=SUBLANES), :]
            start_piece(j * TOP_K, 1 - slot)
            for k in range(1, TOP_K):
                acc = acc + gk[k] * bufs[slot][pl.ds(k * TF * SUBLANES + j, TF, stride=SUBLANES), :]
                start_piece(j * TOP_K + k, 1 - slot)
            cols.append(acc)
        h2 = h1_ref[...] + jnp.concatenate(cols, axis=1)
        gate = jax.nn.sigmoid(jnp.dot(_rms(h2, gple_ref[...]).astype(BF16), wg_ref[...],
                                      preferred_element_type=F32))
        ple = _rms(jnp.dot(p_ref[...].astype(BF16), wp_ref[...], preferred_element_type=F32),
                   gpost_ref[...])
        out_ref[...] = _rms(h2 + gate * ple, gfin_ref[...])

    @pl.when(step == 0)
    def _():
        gather(dest_ref, 0, start=True)

    for slot in range(2):
        @pl.when(step % 2 == slot)
        def _(slot=slot):
            gather(dest_ref, slot, start=False)
            compute(slot)

            @pl.when(step + 1 == n_steps)
            def _():
                gather(dest_next_ref, 1 - slot, start=False)


def _final(dest_flat, h1, gates, p2, y_rows, ple_norm, w_ple_gate, w_ple_proj, ple_post_norm, final_norm):
    N, D = h1.shape
    const = lambda i: (0, 0)
    n_steps = N // TF
    return pl.pallas_call(
        _final_kernel,
        grid=(n_steps,),
        in_specs=[
            pl.BlockSpec((TF * TOP_K,), lambda i: (i,), memory_space=pltpu.SMEM),
            pl.BlockSpec((TF * TOP_K,), lambda i: (jnp.minimum(i + 1, n_steps - 1),),
                         memory_space=pltpu.SMEM),
            pl.BlockSpec((TF, D), lambda i: (i, 0)),
            pl.BlockSpec((TF, TOP_K), lambda i: (i, 0)),
            pl.BlockSpec((TF, PLE_DIM), lambda i: (i, 0)),
            pl.BlockSpec(memory_space=pl.ANY),
            pl.BlockSpec(ple_norm.shape, const),
            pl.BlockSpec(w_ple_gate.shape, const),
            pl.BlockSpec(w_ple_proj.shape, const),
            pl.BlockSpec(ple_post_norm.shape, const),
            pl.BlockSpec(final_norm.shape, const),
        ],
        out_specs=pl.BlockSpec((TF, D), lambda i: (i, 0)),
        out_shape=jax.ShapeDtypeStruct((N, D), F32),
        scratch_shapes=[pltpu.VMEM((TOP_K * TF * SUBLANES, LANES), F32),
                        pltpu.VMEM((TOP_K * TF * SUBLANES, LANES), F32),
                        pltpu.SemaphoreType.DMA((2,))],
        compiler_params=pltpu.CompilerParams(
            dimension_semantics=("arbitrary",), vmem_limit_bytes=VMEM_LIMIT),
        name="combine_ple_final",
    )(dest_flat, dest_flat, h1, gates, p2, y_rows, ple_norm, w_ple_gate, w_ple_proj, ple_post_norm,
      final_norm)


def _reorder_w_in(w):
    o = Q_LORA_RANK + KV_LORA_RANK
    return jnp.concatenate([w[:, :o], w[:, o + QK_ROPE_DIM:], w[:, o:o + QK_ROPE_DIM]], axis=1)


def _reorder_w_uq(w):
    w4 = w.reshape(Q_LORA_RANK, N_HEADS, QK_DIM)
    nope = w4[:, :, :QK_NOPE_DIM].reshape(Q_LORA_RANK, N_HEADS * QK_NOPE_DIM)
    r1 = w4[:, :, QK_NOPE_DIM:QK_NOPE_DIM + ROPE_HALF].reshape(Q_LORA_RANK, N_HEADS * ROPE_HALF)
    r2 = w4[:, :, QK_NOPE_DIM + ROPE_HALF:].reshape(Q_LORA_RANK, N_HEADS * ROPE_HALF)
    return jnp.concatenate([nope, r1, r2], axis=1)


def kernel(x, p, positions, mix_norm, w_in, q_norm, w_uq, kv_norm, w_ukv, conv_w, attn_out_norm, conv_out_norm, w_o, moe_norm, w_router, b_router, w1, b1, w2, b2, ple_norm, w_ple_gate, w_ple_proj, ple_post_norm, final_norm):
    B, S, D = x.shape
    N = B * S
    assert p.shape[0] == 1, "single layer"
    half = ROPE_HALF
    inv_freq = ROPE_THETA ** (-jnp.arange(half, dtype=F32) / half)
    invf_col = inv_freq[:, None]

    q, k, v, conv_n = _inproj(
        x, positions[:, None, :], invf_col, mix_norm, _reorder_w_in(w_in[0]).astype(BF16), q_norm,
        _reorder_w_uq(w_uq[0]).astype(BF16), kv_norm, w_ukv[0].astype(BF16), conv_w[0], conv_out_norm)
    attn = _attention(q, k, v)

    h1, hn_rows, route, gates, counts = _outproj(
        x.reshape(N, D), attn.reshape(N, ATTN_WIDTH), conv_n.reshape(N, CONV_WIDTH), attn_out_norm,
        w_o[0].astype(BF16), moe_norm, w_router[0], b_router)

    counts = counts[0]
    padded = ((counts + TMF - 1) // TMF) * TMF
    pad_ends = jnp.cumsum(padded)
    pad_starts = pad_ends - padded
    n_blocks = (N * TOP_K) // TMF + N_EXPERTS
    dest = pad_starts[route[:, :TOP_K]] + route[:, TOP_K:]
    dest_flat = dest.reshape(N * TOP_K).astype(jnp.int32)
    blk_start = jnp.arange(n_blocks, dtype=jnp.int32) * TMF
    blk_expert = jnp.minimum(
        jnp.sum((pad_ends[None, :] <= blk_start[:, None]).astype(jnp.int32), axis=1), N_EXPERTS - 1)
    n_used = (pad_ends[-1:] // TMF).astype(jnp.int32)

    total_rows = jnp.full((1,), n_blocks * TMF, jnp.int32)
    fill_lo = jnp.concatenate([pad_starts + counts, pad_ends[-1:]]).astype(jnp.int32)
    fill_hi = jnp.concatenate([pad_ends, total_rows]).astype(jnp.int32)
    xs_rows = _dispatch(fill_lo, fill_hi, dest_flat, hn_rows, n_blocks * TMF)
    y_rows = _ffn(blk_expert, n_used, xs_rows, w1[0], b1[0][:, None, :], w2[0], b2[0][:, None, :])

    out = _final(dest_flat, h1, gates, p[0].reshape(N, PLE_DIM), y_rows, ple_norm,
                 w_ple_gate[0].astype(BF16), w_ple_proj[0].astype(BF16), ple_post_norm,
                 final_norm.reshape(1, D))
    return out.reshape(B, S, D)
```

```python
import functools

import jax
import jax.numpy as jnp
from jax import lax
from jax.experimental import pallas as pl
from jax.experimental.pallas import tpu as pltpu

D_MODEL = 1024
PLE_DIM = 256
N_HEADS = 4
QK_NOPE_DIM = 128
QK_ROPE_DIM = 64
ROPE_HALF = QK_ROPE_DIM // 2
V_HEAD_DIM = 128
V_EXT = 2 * V_HEAD_DIM
QK_DIM = QK_NOPE_DIM + QK_ROPE_DIM
Q_LORA_RANK = 256
KV_LORA_RANK = 128
ROPE_THETA = 10000.0
ATTN_WIDTH = N_HEADS * V_HEAD_DIM
CONV_WIDTH = 512
CONV_K = 3
N_EXPERTS = 32
TOP_K = 4
D_EXPERT = 1024
SWIGLU_ALPHA = 1.702
SWIGLU_LIMIT = 7.0
EPS = 1e-6

SUBLANES = 8
LANES = 128
ROW_CHUNKS = D_MODEL // LANES

TM_IN = 512
TQ = 512
TK = 512
ATT_ROW_SPLIT = 2
TM_OUT = 512
TD = 512
DISPATCH_CHUNK = 64
ZERO_ROWS = 64
TMF = 512
W_CAST_ROWS = 256
TF = 256

VMEM_LIMIT = 56 * 1024 * 1024

BF16 = jnp.bfloat16
F32 = jnp.float32


def _rms(x, g):
    return x * lax.rsqrt(jnp.mean(x * x, axis=-1, keepdims=True) + EPS) * g


def _inproj_kernel(x_ref, pos_ref, invf_ref, gmix_ref, win_ref, gq_ref, wuq_ref, gkv_ref, wukv_ref,
                   convw_ref, gconv_ref, q_ref, k_ref, v_ref, conv_ref, ubuf_ref):
    tm = x_ref.shape[1]
    s_idx = pl.program_id(1)

    xn = _rms(x_ref[0], gmix_ref[...])
    z = jnp.dot(xn.astype(BF16), win_ref[...], preferred_element_type=F32)
    c_q = z[:, 0:Q_LORA_RANK]
    c_kv = z[:, Q_LORA_RANK:Q_LORA_RANK + KV_LORA_RANK]
    o = Q_LORA_RANK + KV_LORA_RANK
    b_gate = z[:, o:o + CONV_WIDTH]
    c_gate = z[:, o + CONV_WIDTH:o + 2 * CONV_WIDTH]
    val = z[:, o + 2 * CONV_WIDTH:o + 3 * CONV_WIDTH]
    k_pe = z[:, o + 3 * CONV_WIDTH:o + 3 * CONV_WIDTH + QK_ROPE_DIM]

    ang_t = invf_ref[...] * pos_ref[0].astype(F32)
    cos_t = jnp.cos(ang_t)
    sin_t = jnp.sin(ang_t)
    cos4 = jnp.concatenate([cos_t] * N_HEADS, axis=0).T
    sin4 = jnp.concatenate([sin_t] * N_HEADS, axis=0).T

    q = jnp.dot(_rms(c_q, gq_ref[...]).astype(BF16), wuq_ref[...], preferred_element_type=F32)
    qn = N_HEADS * QK_NOPE_DIM
    x1 = q[:, qn:qn + LANES]
    x2 = q[:, qn + LANES:qn + 2 * LANES]
    o1 = x1 * cos4 - x2 * sin4
    o2 = x1 * sin4 + x2 * cos4
    scale = QK_DIM ** -0.5

    kv = jnp.dot(_rms(c_kv, gkv_ref[...]).astype(BF16), wukv_ref[...], preferred_element_type=F32)
    k1 = k_pe[:, :ROPE_HALF]
    k2 = k_pe[:, ROPE_HALF:]
    c1 = cos4[:, :ROPE_HALF]
    s1 = sin4[:, :ROPE_HALF]
    kr = jnp.concatenate([k1 * c1 - k2 * s1, k1 * s1 + k2 * c1], axis=1)

    for h in range(N_HEADS):
        qh = jnp.concatenate(
            [q[:, h * QK_NOPE_DIM:(h + 1) * QK_NOPE_DIM],
             o1[:, h * ROPE_HALF:(h + 1) * ROPE_HALF],
             o2[:, h * ROPE_HALF:(h + 1) * ROPE_HALF]], axis=1) * scale
        q_ref[0, h] = qh.astype(BF16)
        base = h * (QK_NOPE_DIM + V_HEAD_DIM)
        kh = jnp.concatenate([kv[:, base:base + QK_NOPE_DIM], kr], axis=1)
        k_ref[0, h] = kh.astype(BF16)
        vh = kv[:, base + QK_NOPE_DIM:base + QK_NOPE_DIM + V_HEAD_DIM]
        v_ref[0, h] = jnp.concatenate([vh, jnp.ones_like(vh)], axis=1).astype(BF16)

    u = c_gate * val

    @pl.when(s_idx == 0)
    def _():
        ubuf_ref[0:SUBLANES, :] = jnp.zeros((SUBLANES, CONV_WIDTH), F32)

    ubuf_ref[SUBLANES:SUBLANES + tm, :] = u
    u_m1 = ubuf_ref[SUBLANES - 1:SUBLANES - 1 + tm, :]
    u_m2 = ubuf_ref[SUBLANES - 2:SUBLANES - 2 + tm, :]
    cw = convw_ref[...]
    y = cw[0:1, :] * u_m2 + cw[1:2, :] * u_m1 + cw[2:3, :] * u
    conv_ref[0] = _rms(b_gate * y, gconv_ref[...]).astype(BF16)
    ubuf_ref[0:SUBLANES, :] = ubuf_ref[tm:tm + SUBLANES, :]


def _inproj(x, positions, invf4, mix_norm, w_in_r, q_norm, w_uq_r, kv_norm, w_ukv, conv_w, conv_out_norm):
    B, S, D = x.shape
    tm = TM_IN
    const = lambda b, s: (0, 0)
    return pl.pallas_call(
        _inproj_kernel,
        grid=(B, S // tm),
        in_specs=[
            pl.BlockSpec((1, tm, D), lambda b, s: (b, s, 0)),
            pl.BlockSpec((1, 1, tm), lambda b, s: (b, 0, s)),
            pl.BlockSpec(invf4.shape, const),
            pl.BlockSpec(mix_norm.shape, const),
            pl.BlockSpec(w_in_r.shape, const),
            pl.BlockSpec(q_norm.shape, const),
            pl.BlockSpec(w_uq_r.shape, const),
            pl.BlockSpec(kv_norm.shape, const),
            pl.BlockSpec(w_ukv.shape, const),
            pl.BlockSpec(conv_w.shape, const),
            pl.BlockSpec(conv_out_norm.shape, const),
        ],
        out_specs=[
            pl.BlockSpec((1, N_HEADS, tm, QK_DIM), lambda b, s: (b, 0, s, 0)),
            pl.BlockSpec((1, N_HEADS, tm, QK_DIM), lambda b, s: (b, 0, s, 0)),
            pl.BlockSpec((1, N_HEADS, tm, V_EXT), lambda b, s: (b, 0, s, 0)),
            pl.BlockSpec((1, tm, CONV_WIDTH), lambda b, s: (b, s, 0)),
        ],
        out_shape=[
            jax.ShapeDtypeStruct((B, N_HEADS, S, QK_DIM), BF16),
            jax.ShapeDtypeStruct((B, N_HEADS, S, QK_DIM), BF16),
            jax.ShapeDtypeStruct((B, N_HEADS, S, V_EXT), BF16),
            jax.ShapeDtypeStruct((B, S, CONV_WIDTH), BF16),
        ],
        scratch_shapes=[pltpu.VMEM((tm + 2 * SUBLANES, CONV_WIDTH), F32)],
        compiler_params=pltpu.CompilerParams(
            dimension_semantics=("arbitrary", "arbitrary"), vmem_limit_bytes=VMEM_LIMIT),
        name="inproj",
    )(x, positions, invf4, mix_norm, w_in_r, q_norm, w_uq_r, kv_norm, w_ukv, conv_w, conv_out_norm)


def _attn_kernel(q_ref, k_ref, v_ref, o_ref):
    tq = q_ref.shape[2]
    i = pl.program_id(1)
    neg = jnp.float32(-1e30)

    rq = tq // ATT_ROW_SPLIT
    chains = [(h, r) for h in range(N_HEADS) for r in range(ATT_ROW_SPLIT)]

    def step(j, carry, col_off=None):
        start = pl.multiple_of(j * TK, TK)
        new = []
        for c, (h, r) in enumerate(chains):
            m, acc = carry[c]
            if col_off is not None and col_off >= (r + 1) * rq:
                new.append((m, acc))
                continue
            kj = k_ref[0, h, pl.ds(start, TK), :]
            vj = v_ref[0, h, pl.ds(start, TK), :]
            s = lax.dot_general(q_ref[0, h, r * rq:(r + 1) * rq, :], kj, (((1,), (1,)), ((), ())),
                                preferred_element_type=F32)
            if col_off is not None and col_off + TK - 1 > r * rq:
                row = lax.broadcasted_iota(jnp.int32, (rq, TK), 0) + r * rq
                col = lax.broadcasted_iota(jnp.int32, (rq, TK), 1) + col_off
                s = jnp.where(col <= row, s, neg)
            m_new = jnp.maximum(m, jnp.max(s, axis=-1, keepdims=True))
            alpha = jnp.exp(m - m_new)
            p = jnp.exp(s - m_new)
            acc = alpha * acc + jnp.dot(p.astype(BF16), vj, preferred_element_type=F32)
            new.append((m_new, acc))
        return tuple(new)

    init = tuple((jnp.full((rq, 1), neg, F32), jnp.zeros((rq, V_EXT), F32)) for _ in chains)
    ratio = tq // TK
    final = lax.fori_loop(0, i * ratio, lambda j, c: step(j, c), init)
    for d in range(ratio):
        final = step(i * ratio + d, final, col_off=d * TK)
    for c, (h, r) in enumerate(chains):
        acc = final[c][1]
        o_ref[0, r * rq:(r + 1) * rq, h * V_HEAD_DIM:(h + 1) * V_HEAD_DIM] = (
            acc[:, :V_HEAD_DIM] / acc[:, V_HEAD_DIM:V_HEAD_DIM + 1]).astype(BF16)


def _attention(q, k, v):
    B, H, S, _ = q.shape
    assert TQ % TK == 0 and TQ % ATT_ROW_SPLIT == 0
    return pl.pallas_call(
        _attn_kernel,
        grid=(B, S // TQ),
        in_specs=[
            pl.BlockSpec((1, H, TQ, QK_DIM), lambda b, i: (b, 0, i, 0)),
            pl.BlockSpec((1, H, S, QK_DIM), lambda b, i: (b, 0, 0, 0)),
            pl.BlockSpec((1, H, S, V_EXT), lambda b, i: (b, 0, 0, 0)),
        ],
        out_specs=pl.BlockSpec((1, TQ, H * V_HEAD_DIM), lambda b, i: (b, i, 0)),
        out_shape=jax.ShapeDtypeStruct((B, S, H * V_HEAD_DIM), BF16),
        compiler_params=pltpu.CompilerParams(
            dimension_semantics=("arbitrary", "arbitrary"), vmem_limit_bytes=VMEM_LIMIT),
        name="attention",
    )(q, k, v)


def _outproj_kernel(x_ref, attn_ref, conv_ref, ga_ref, wo_ref, gmoe_ref, wr_hi_ref, wr_lo_ref, br_ref,
                    h1_ref, hn_ref, route_ref, gate_ref, counts_ref, carry_ref):
    tm = x_ref.shape[0]
    step = pl.program_id(0)

    @pl.when(step == 0)
    def _():
        carry_ref[...] = jnp.zeros_like(carry_ref)

    attn_n = _rms(attn_ref[...].astype(F32), ga_ref[...]).astype(BF16)
    mixed = jnp.concatenate([attn_n, conv_ref[...]], axis=1)
    h1 = x_ref[...] + jnp.dot(mixed, wo_ref[...], preferred_element_type=F32)
    h1_ref[...] = h1

    hn = _rms(h1, gmoe_ref[...])
    for j in range(ROW_CHUNKS):
        hn_ref[pl.ds(j, tm, stride=SUBLANES), :] = hn[:, j * LANES:(j + 1) * LANES]

    hn_hi = hn.astype(BF16)
    hn_lo = (hn - hn_hi.astype(F32)).astype(BF16)
    logits = (jnp.dot(hn_hi, wr_hi_ref[...], preferred_element_type=F32)
              + jnp.dot(hn_hi, wr_lo_ref[...], preferred_element_type=F32)
              + jnp.dot(hn_lo, wr_hi_ref[...], preferred_element_type=F32)) + br_ref[...]

    lane = lax.broadcasted_iota(jnp.int32, (tm, N_EXPERTS), 1).astype(F32)
    work = logits
    vals, idxs, hots = [], [], []
    for _ in range(TOP_K):
        mx = jnp.max(work, axis=-1, keepdims=True)
        idx = jnp.min(jnp.where(work == mx, lane, float(N_EXPERTS)), axis=-1, keepdims=True)
        hot = lane == idx
        work = jnp.where(hot, -jnp.inf, work)
        vals.append(mx)
        idxs.append(idx)
        hots.append(hot)

    exps = [jnp.exp(v - vals[0]) for v in vals]
    denom = exps[0] + exps[1] + exps[2] + exps[3]
    gates = [e / denom for e in exps]

    sel = (hots[0] | hots[1] | hots[2] | hots[3]).astype(F32)
    r = lax.broadcasted_iota(jnp.int32, (tm, tm), 0)
    c = lax.broadcasted_iota(jnp.int32, (tm, tm), 1)
    tri = (c < r).astype(BF16)
    rank_mat = jnp.dot(tri, sel.astype(BF16), preferred_element_type=F32) + carry_ref[...]
    carry_ref[...] = carry_ref[...] + jnp.sum(sel, axis=0, keepdims=True)
    counts_ref[...] = carry_ref[...].astype(jnp.int32)

    ranks = [jnp.sum(jnp.where(h, rank_mat, 0.0), axis=-1, keepdims=True) for h in hots]
    lane = lax.broadcasted_iota(jnp.int32, (tm, LANES), 1)
    route = jnp.zeros((tm, LANES), F32)
    gate_out = jnp.zeros((tm, TOP_K), F32)
    lane4 = lax.broadcasted_iota(jnp.int32, (tm, TOP_K), 1)
    for k in range(TOP_K):
        route = jnp.where(lane == k, idxs[k], route)
        route = jnp.where(lane == TOP_K + k, ranks[k], route)
        gate_out = jnp.where(lane4 == k, gates[k], gate_out)
    route_ref[...] = route.T[:2 * TOP_K, :].astype(jnp.int32)
    gate_ref[...] = gate_out


def _outproj(x2, attn2, conv2, attn_out_norm, w_o, moe_norm, w_router, b_router):
    N, D = x2.shape
    wr_hi = w_router.astype(BF16)
    wr_lo = (w_router - wr_hi.astype(F32)).astype(BF16)
    tm = TM_OUT
    const = lambda i: (0, 0)
    return pl.pallas_call(
        _outproj_kernel,
        grid=(N // tm,),
        in_specs=[
            pl.BlockSpec((tm, D), lambda i: (i, 0)),
            pl.BlockSpec((tm, ATTN_WIDTH), lambda i: (i, 0)),
            pl.BlockSpec((tm, CONV_WIDTH), lambda i: (i, 0)),
            pl.BlockSpec(attn_out_norm.shape, const),
            pl.BlockSpec(w_o.shape, const),
            pl.BlockSpec(moe_norm.shape, const),
            pl.BlockSpec(w_router.shape, const),
            pl.BlockSpec(w_router.shape, const),
            pl.BlockSpec(b_router.shape, const),
        ],
        out_specs=[
            pl.BlockSpec((tm, D), lambda i: (i, 0)),
            pl.BlockSpec((tm * SUBLANES, LANES), lambda i: (i, 0)),
            pl.BlockSpec((2 * TOP_K, tm), lambda i: (0, i)),
            pl.BlockSpec((tm, TOP_K), lambda i: (i, 0)),
            pl.BlockSpec((1, N_EXPERTS), const),
        ],
        out_shape=[
            jax.ShapeDtypeStruct((N, D), F32),
            jax.ShapeDtypeStruct((N * SUBLANES, LANES), F32),
            jax.ShapeDtypeStruct((2 * TOP_K, N), jnp.int32),
            jax.ShapeDtypeStruct((N, TOP_K), F32),
            jax.ShapeDtypeStruct((1, N_EXPERTS), jnp.int32),
        ],
        scratch_shapes=[pltpu.VMEM((1, N_EXPERTS), F32)],
        compiler_params=pltpu.CompilerParams(
            dimension_semantics=("arbitrary",), vmem_limit_bytes=VMEM_LIMIT),
        name="outproj_router",
    )(x2, attn2, conv2, attn_out_norm, w_o, moe_norm, wr_hi, wr_lo, b_router)


def _row_copy(src_ref, src_row, dst_ref, dst_row, sem):
    s = pl.multiple_of(src_row * SUBLANES, SUBLANES)
    d = pl.multiple_of(dst_row * SUBLANES, SUBLANES)
    return pltpu.make_async_copy(src_ref.at[pl.ds(s, SUBLANES), :], dst_ref.at[pl.ds(d, SUBLANES), :], sem)


def _dispatch_kernel(fill_lo_ref, fill_hi_ref, dest_ref, hn_ref, zero_ref, xs_ref, sems, zsem):
    step = pl.program_id(0)

    def zero_copy(first_row, n_rows):
        d = pl.multiple_of(first_row * SUBLANES, SUBLANES)
        return pltpu.make_async_copy(zero_ref.at[pl.ds(0, n_rows * SUBLANES), :],
                                     xs_ref.at[pl.ds(d, n_rows * SUBLANES), :], zsem)

    def fill_segments(start):
        def act(cp):
            if start:
                cp.start()
            else:
                cp.wait()

        def per_segment(e, _):
            lo = fill_lo_ref[e]
            n = fill_hi_ref[e] - lo
            n_full = n // ZERO_ROWS

            def full(c, _):
                act(zero_copy(lo + c * ZERO_ROWS, ZERO_ROWS))
                return 0
            lax.fori_loop(0, n_full, full, 0)
            off = lo + n_full * ZERO_ROWS
            rem = n - n_full * ZERO_ROWS
            size = ZERO_ROWS // 2
            while size >= 1:
                @pl.when((rem & size) != 0)
                def _(size=size):
                    act(zero_copy(off + (rem & ~(2 * size - 1)), size))
                size //= 2
            return 0
        lax.fori_loop(0, N_EXPERTS + 1, per_segment, 0)

    @pl.when(step == 0)
    def _():
        fill_segments(start=True)
        fill_segments(start=False)

    def batch(c, slot, start):
        def body(t, _):
            tok = c * DISPATCH_CHUNK + t
            for k in range(TOP_K):
                cp = _row_copy(hn_ref, tok, xs_ref, dest_ref[k, tok], sems.at[slot])
                if start:
                    cp.start(priority=k % 2)
                else:
                    cp.wait()
            return 0
        lax.fori_loop(0, DISPATCH_CHUNK, body, 0)

    def pipelined(cc, _):
        even = 2 * cc
        batch(even, 0, start=True)

        @pl.when(cc > 0)
        def _():
            batch(even - 1, 1, start=False)
        batch(even + 1, 1, start=True)
        batch(even, 0, start=False)
        return 0

    n_batches = TD // DISPATCH_CHUNK
    lax.fori_loop(0, n_batches // 2, pipelined, 0)
    batch(n_batches - 1, 1, start=False)


def _dispatch(fill_lo, fill_hi, dest_t, hn_rows, n_rows_padded):
    N = hn_rows.shape[0] // SUBLANES
    zero_row = jnp.zeros((ZERO_ROWS * SUBLANES, LANES), F32)
    grid_spec = pltpu.PrefetchScalarGridSpec(
        num_scalar_prefetch=2,
        grid=(N // TD,),
        in_specs=[
            pl.BlockSpec((TOP_K, TD), lambda i, *_: (0, i), memory_space=pltpu.SMEM),
            pl.BlockSpec((TD * SUBLANES, LANES), lambda i, *_: (i, 0)),
            pl.BlockSpec((ZERO_ROWS * SUBLANES, LANES), lambda i, *_: (0, 0)),
        ],
        out_specs=pl.BlockSpec(memory_space=pl.ANY),
        scratch_shapes=[pltpu.SemaphoreType.DMA((2,)), pltpu.SemaphoreType.DMA(())],
    )
    return pl.pallas_call(
        _dispatch_kernel,
        grid_spec=grid_spec,
        out_shape=jax.ShapeDtypeStruct((n_rows_padded * SUBLANES, LANES), F32),
        compiler_params=pltpu.CompilerParams(
            dimension_semantics=("arbitrary",), has_side_effects=True),
        name="dispatch",
    )(fill_lo, fill_hi, dest_t, hn_rows, zero_row)


def _ffn_kernel(blk_e_ref, nblk_ref, xs_ref, w1_ref, b1_ref, w2_ref, b2_ref, y_ref, w1b_ref, w2b_ref):
    i = pl.program_id(0)
    used = i < nblk_ref[0]

    @pl.when(used & ((i == 0) | (blk_e_ref[i] != blk_e_ref[jnp.maximum(i - 1, 0)])))
    def _():
        for r in range(0, D_MODEL, W_CAST_ROWS):
            w1b_ref[r:r + W_CAST_ROWS, :] = w1_ref[0, r:r + W_CAST_ROWS, :].astype(BF16)
        for r in range(0, D_EXPERT, W_CAST_ROWS):
            w2b_ref[r:r + W_CAST_ROWS, :] = w2_ref[0, r:r + W_CAST_ROWS, :].astype(BF16)

    @pl.when(used)
    def _():
        xb = jnp.concatenate(
            [xs_ref[pl.ds(j, TMF, stride=SUBLANES), :] for j in range(ROW_CHUNKS)], axis=1)
        h = jnp.dot(xb.astype(BF16), w1b_ref[...], preferred_element_type=F32) + b1_ref[0]
        gate = jnp.minimum(h[:, :D_EXPERT], SWIGLU_LIMIT)
        up = jnp.clip(h[:, D_EXPERT:], -SWIGLU_LIMIT, SWIGLU_LIMIT)
        act = gate * jax.nn.sigmoid(SWIGLU_ALPHA * gate) * (up + 1.0)
        y = jnp.dot(act.astype(BF16), w2b_ref[...], preferred_element_type=F32) + b2_ref[0]
        for j in range(ROW_CHUNKS):
            y_ref[pl.ds(j, TMF, stride=SUBLANES), :] = y[:, j * LANES:(j + 1) * LANES]

    @pl.when(i >= nblk_ref[0])
    def _():
        y_ref[...] = jnp.zeros_like(y_ref)


def _ffn(blk_expert, n_used, xs_rows, w1, b1, w2, b2):
    n_blocks = blk_expert.shape[0]

    def row_map(i, be, nb):
        return (i, 0)

    def w_map(i, be, nb):
        return (be[i], 0, 0)

    grid_spec = pltpu.PrefetchScalarGridSpec(
        num_scalar_prefetch=2,
        grid=(n_blocks,),
        in_specs=[
            pl.BlockSpec((TMF * SUBLANES, LANES), row_map),
            pl.BlockSpec((1, D_MODEL, 2 * D_EXPERT), w_map),
            pl.BlockSpec((1, 1, 2 * D_EXPERT), w_map),
            pl.BlockSpec((1, D_EXPERT, D_MODEL), w_map),
            pl.BlockSpec((1, 1, D_MODEL), w_map),
        ],
        out_specs=pl.BlockSpec((TMF * SUBLANES, LANES), row_map),
        scratch_shapes=[pltpu.VMEM((D_MODEL, 2 * D_EXPERT), BF16), pltpu.VMEM((D_EXPERT, D_MODEL), BF16)],
    )
    return pl.pallas_call(
        _ffn_kernel,
        grid_spec=grid_spec,
        out_shape=jax.ShapeDtypeStruct(xs_rows.shape, F32),
        compiler_params=pltpu.CompilerParams(
            dimension_semantics=("arbitrary",), vmem_limit_bytes=VMEM_LIMIT),
        name="expert_ffn",
    )(blk_expert, n_used, xs_rows, w1, b1, w2, b2)


def _final_kernel(dest_ref, dest_next_ref, h1_ref, gate_ref, p_ref, y_ref, gple_ref, wg_ref, wp_ref,
                  gpost_ref, gfin_ref, out_ref, ybuf0_ref, ybuf1_ref, sems):
    step = pl.program_id(0)
    n_steps = pl.num_programs(0)
    bufs = (ybuf0_ref, ybuf1_ref)

    def gather(idx_ref, slot, start):
        def body(t, _):
            for k in range(TOP_K):
                cp = _row_copy(y_ref, idx_ref[k, t], bufs[slot], k * TF + t, sems.at[slot])
                if start:
                    cp.start(priority=k % 2)
                else:
                    cp.wait()
            return 0
        lax.fori_loop(0, TF, body, 0)

    tokens_per_piece = TF // (ROW_CHUNKS * TOP_K)

    def start_piece(piece, slot):
        for t in range(piece * tokens_per_piece, (piece + 1) * tokens_per_piece):
            for k in range(TOP_K):
                _row_copy(y_ref, dest_next_ref[k, t], bufs[slot], k * TF + t,
                          sems.at[slot]).start(priority=k % 2)

    def compute(slot):
        g = gate_ref[...]
        gk = [jnp.broadcast_to(g[:, k:k + 1], (TF, LANES)) for k in range(TOP_K)]
        cols = []
        for j in range(ROW_CHUNKS):
            acc = gk[0] * bufs[slot][pl.ds(j, TF, stride=SUBLANES), :]
            start_piece(j * TOP_K, 1 - slot)
            for k in range(1, TOP_K):
                acc = acc + gk[k] * bufs[slot][pl.ds(k * TF * SUBLANES + j, TF, stride=SUBLANES), :]
                start_piece(j * TOP_K + k, 1 - slot)
            cols.append(acc)
        h2 = h1_ref[...] + jnp.concatenate(cols, axis=1)
        gate = jax.nn.sigmoid(jnp.dot(_rms(h2, gple_ref[...]).astype(BF16), wg_ref[...],
                                      preferred_element_type=F32))
        ple = _rms(jnp.dot(p_ref[...].astype(BF16), wp_ref[...], preferred_element_type=F32),
                   gpost_ref[...])
        out_ref[...] = _rms(h2 + gate * ple, gfin_ref[...])

    @pl.when(step == 0)
    def _():
        gather(dest_ref, 0, start=True)

    for slot in range(2):
        @pl.when(step % 2 == slot)
        def _(slot=slot):
            gather(dest_ref, slot, start=False)
            compute(slot)

            @pl.when(step + 1 == n_steps)
            def _():
                gather(dest_next_ref, 1 - slot, start=False)


def _final(dest_t, h1, gates, p2, y_rows, ple_norm, w_ple_gate, w_ple_proj, ple_post_norm, final_norm):
    N, D = h1.shape
    const = lambda i: (0, 0)
    n_steps = N // TF
    return pl.pallas_call(
        _final_kernel,
        grid=(n_steps,),
        in_specs=[
            pl.BlockSpec((TOP_K, TF), lambda i: (0, i), memory_space=pltpu.SMEM),
            pl.BlockSpec((TOP_K, TF), lambda i: (0, jnp.minimum(i + 1, n_steps - 1)),
                         memory_space=pltpu.SMEM),
            pl.BlockSpec((TF, D), lambda i: (i, 0)),
            pl.BlockSpec((TF, TOP_K), lambda i: (i, 0)),
            pl.BlockSpec((TF, PLE_DIM), lambda i: (i, 0)),
            pl.BlockSpec(memory_space=pl.ANY),
            pl.BlockSpec(ple_norm.shape, const),
            pl.BlockSpec(w_ple_gate.shape, const),
            pl.BlockSpec(w_ple_proj.shape, const),
            pl.BlockSpec(ple_post_norm.shape, const),
            pl.BlockSpec(final_norm.shape, const),
        ],
        out_specs=pl.BlockSpec((TF, D), lambda i: (i, 0)),
        out_shape=jax.ShapeDtypeStruct((N, D), F32),
        scratch_shapes=[pltpu.VMEM((TOP_K * TF * SUBLANES, LANES), F32),
                        pltpu.VMEM((TOP_K * TF * SUBLANES, LANES), F32),
                        pltpu.SemaphoreType.DMA((2,))],
        compiler_params=pltpu.CompilerParams(
            dimension_semantics=("arbitrary",), vmem_limit_bytes=VMEM_LIMIT),
        name="combine_ple_final",
    )(dest_t, dest_t, h1, gates, p2, y_rows, ple_norm, w_ple_gate, w_ple_proj, ple_post_norm,
      final_norm)


def _reorder_w_in(w):
    o = Q_LORA_RANK + KV_LORA_RANK
    return jnp.concatenate([w[:, :o], w[:, o + QK_ROPE_DIM:], w[:, o:o + QK_ROPE_DIM]], axis=1)


def _reorder_w_uq(w):
    w4 = w.reshape(Q_LORA_RANK, N_HEADS, QK_DIM)
    nope = w4[:, :, :QK_NOPE_DIM].reshape(Q_LORA_RANK, N_HEADS * QK_NOPE_DIM)
    r1 = w4[:, :, QK_NOPE_DIM:QK_NOPE_DIM + ROPE_HALF].reshape(Q_LORA_RANK, N_HEADS * ROPE_HALF)
    r2 = w4[:, :, QK_NOPE_DIM + ROPE_HALF:].reshape(Q_LORA_RANK, N_HEADS * ROPE_HALF)
    return jnp.concatenate([nope, r1, r2], axis=1)


def kernel(x, p, positions, mix_norm, w_in, q_norm, w_uq, kv_norm, w_ukv, conv_w, attn_out_norm, conv_out_norm, w_o, moe_norm, w_router, b_router, w1, b1, w2, b2, ple_norm, w_ple_gate, w_ple_proj, ple_post_norm, final_norm):
    B, S, D = x.shape
    N = B * S
    assert p.shape[0] == 1, "single layer"
    half = ROPE_HALF
    inv_freq = ROPE_THETA ** (-jnp.arange(half, dtype=F32) / half)
    invf_col = inv_freq[:, None]

    q, k, v, conv_n = _inproj(
        x, positions[:, None, :], invf_col, mix_norm, _reorder_w_in(w_in[0]).astype(BF16), q_norm,
        _reorder_w_uq(w_uq[0]).astype(BF16), kv_norm, w_ukv[0].astype(BF16), conv_w[0], conv_out_norm)
    attn = _attention(q, k, v)

    h1, hn_rows, route, gates, counts = _outproj(
        x.reshape(N, D), attn.reshape(N, ATTN_WIDTH), conv_n.reshape(N, CONV_WIDTH), attn_out_norm,
        w_o[0].astype(BF16), moe_norm, w_router[0], b_router)

    counts = counts[0]
    padded = ((counts + TMF - 1) // TMF) * TMF
    pad_ends = jnp.cumsum(padded)
    pad_starts = pad_ends - padded
    n_blocks = (N * TOP_K) // TMF + N_EXPERTS
    dest_t = (pad_starts[route[:TOP_K]] + route[TOP_K:]).astype(jnp.int32)
    blk_start = jnp.arange(n_blocks, dtype=jnp.int32) * TMF
    blk_expert = jnp.minimum(
        jnp.sum((pad_ends[None, :] <= blk_start[:, None]).astype(jnp.int32), axis=1), N_EXPERTS - 1)
    n_used = (pad_ends[-1:] // TMF).astype(jnp.int32)

    total_rows = jnp.full((1,), n_blocks * TMF, jnp.int32)
    fill_lo = jnp.concatenate([pad_starts + counts, pad_ends[-1:]]).astype(jnp.int32)
    fill_hi = jnp.concatenate([pad_ends, total_rows]).astype(jnp.int32)
    xs_rows = _dispatch(fill_lo, fill_hi, dest_t, hn_rows, n_blocks * TMF)
    y_rows = _ffn(blk_expert, n_used, xs_rows, w1[0], b1[0][:, None, :], w2[0], b2[0][:, None, :])

    out = _final(dest_t, h1, gates, p[0].reshape(N, PLE_DIM), y_rows, ple_norm,
                 w_ple_gate[0].astype(BF16), w_ple_proj[0].astype(BF16), ple_post_norm,
                 final_norm.reshape(1, D))
    return out.reshape(B, S, D)
```

```python
import functools

import jax
import jax.numpy as jnp
from jax import lax
from jax.experimental import pallas as pl
from jax.experimental.pallas import tpu as pltpu

D_MODEL = 1024
PLE_DIM = 256
N_HEADS = 4
QK_NOPE_DIM = 128
QK_ROPE_DIM = 64
ROPE_HALF = QK_ROPE_DIM // 2
V_HEAD_DIM = 128
V_EXT = 2 * V_HEAD_DIM
QK_DIM = QK_NOPE_DIM + QK_ROPE_DIM
Q_LORA_RANK = 256
KV_LORA_RANK = 128
ROPE_THETA = 10000.0
ATTN_WIDTH = N_HEADS * V_HEAD_DIM
CONV_WIDTH = 512
CONV_K = 3
N_EXPERTS = 32
TOP_K = 4
D_EXPERT = 1024
SWIGLU_ALPHA = 1.702
SWIGLU_LIMIT = 7.0
EPS = 1e-6

SUBLANES = 8
LANES = 128
ROW_CHUNKS = D_MODEL // LANES

TM_IN = 512
TQ = 512
TK = 512
ATT_ROW_SPLIT = 2
TM_OUT = 512
TD = 512
DISPATCH_CHUNK = 64
ZERO_ROWS = 64
TMF = 512
W_CAST_ROWS = 256
TF = 256

VMEM_LIMIT = 56 * 1024 * 1024

BF16 = jnp.bfloat16
F32 = jnp.float32


def _rms(x, g):
    return x * lax.rsqrt(jnp.mean(x * x, axis=-1, keepdims=True) + EPS) * g


def _inproj_kernel(x_ref, pos_ref, invf_ref, gmix_ref, win_ref, gq_ref, wuq_ref, gkv_ref, wukv_ref,
                   convw_ref, gconv_ref, q_ref, k_ref, v_ref, conv_ref, ubuf_ref):
    tm = x_ref.shape[1]
    s_idx = pl.program_id(1)

    xn = _rms(x_ref[0], gmix_ref[...])
    z = jnp.dot(xn.astype(BF16), win_ref[...], preferred_element_type=F32)
    c_q = z[:, 0:Q_LORA_RANK]
    c_kv = z[:, Q_LORA_RANK:Q_LORA_RANK + KV_LORA_RANK]
    o = Q_LORA_RANK + KV_LORA_RANK
    b_gate = z[:, o:o + CONV_WIDTH]
    c_gate = z[:, o + CONV_WIDTH:o + 2 * CONV_WIDTH]
    val = z[:, o + 2 * CONV_WIDTH:o + 3 * CONV_WIDTH]
    k_pe = z[:, o + 3 * CONV_WIDTH:o + 3 * CONV_WIDTH + QK_ROPE_DIM]

    ang_t = invf_ref[...] * pos_ref[0].astype(F32)
    cos_t = jnp.cos(ang_t)
    sin_t = jnp.sin(ang_t)
    cos4 = jnp.concatenate([cos_t] * N_HEADS, axis=0).T
    sin4 = jnp.concatenate([sin_t] * N_HEADS, axis=0).T

    q = jnp.dot(_rms(c_q, gq_ref[...]).astype(BF16), wuq_ref[...], preferred_element_type=F32)
    qn = N_HEADS * QK_NOPE_DIM
    x1 = q[:, qn:qn + LANES]
    x2 = q[:, qn + LANES:qn + 2 * LANES]
    o1 = x1 * cos4 - x2 * sin4
    o2 = x1 * sin4 + x2 * cos4
    scale = QK_DIM ** -0.5

    kv = jnp.dot(_rms(c_kv, gkv_ref[...]).astype(BF16), wukv_ref[...], preferred_element_type=F32)
    k1 = k_pe[:, :ROPE_HALF]
    k2 = k_pe[:, ROPE_HALF:]
    c1 = cos4[:, :ROPE_HALF]
    s1 = sin4[:, :ROPE_HALF]
    kr = jnp.concatenate([k1 * c1 - k2 * s1, k1 * s1 + k2 * c1], axis=1)

    for h in range(N_HEADS):
        qh = jnp.concatenate(
            [q[:, h * QK_NOPE_DIM:(h + 1) * QK_NOPE_DIM],
             o1[:, h * ROPE_HALF:(h + 1) * ROPE_HALF],
             o2[:, h * ROPE_HALF:(h + 1) * ROPE_HALF]], axis=1) * scale
        q_ref[0, h] = qh.astype(BF16)
        base = h * (QK_NOPE_DIM + V_HEAD_DIM)
        kh = jnp.concatenate([kv[:, base:base + QK_NOPE_DIM], kr], axis=1)
        k_ref[0, h] = kh.astype(BF16)
        vh = kv[:, base + QK_NOPE_DIM:base + QK_NOPE_DIM + V_HEAD_DIM]
        v_ref[0, h] = jnp.concatenate([vh, jnp.ones_like(vh)], axis=1).astype(BF16)

    u = c_gate * val

    @pl.when(s_idx == 0)
    def _():
        ubuf_ref[0:SUBLANES, :] = jnp.zeros((SUBLANES, CONV_WIDTH), F32)

    ubuf_ref[SUBLANES:SUBLANES + tm, :] = u
    u_m1 = ubuf_ref[SUBLANES - 1:SUBLANES - 1 + tm, :]
    u_m2 = ubuf_ref[SUBLANES - 2:SUBLANES - 2 + tm, :]
    cw = convw_ref[...]
    y = cw[0:1, :] * u_m2 + cw[1:2, :] * u_m1 + cw[2:3, :] * u
    conv_ref[0] = _rms(b_gate * y, gconv_ref[...]).astype(BF16)
    ubuf_ref[0:SUBLANES, :] = ubuf_ref[tm:tm + SUBLANES, :]


def _inproj(x, positions, invf4, mix_norm, w_in_r, q_norm, w_uq_r, kv_norm, w_ukv, conv_w, conv_out_norm):
    B, S, D = x.shape
    tm = TM_IN
    const = lambda b, s: (0, 0)
    return pl.pallas_call(
        _inproj_kernel,
        grid=(B, S // tm),
        in_specs=[
            pl.BlockSpec((1, tm, D), lambda b, s: (b, s, 0)),
            pl.BlockSpec((1, 1, tm), lambda b, s: (b, 0, s)),
            pl.BlockSpec(invf4.shape, const),
            pl.BlockSpec(mix_norm.shape, const),
            pl.BlockSpec(w_in_r.shape, const),
            pl.BlockSpec(q_norm.shape, const),
            pl.BlockSpec(w_uq_r.shape, const),
            pl.BlockSpec(kv_norm.shape, const),
            pl.BlockSpec(w_ukv.shape, const),
            pl.BlockSpec(conv_w.shape, const),
            pl.BlockSpec(conv_out_norm.shape, const),
        ],
        out_specs=[
            pl.BlockSpec((1, N_HEADS, tm, QK_DIM), lambda b, s: (b, 0, s, 0)),
            pl.BlockSpec((1, N_HEADS, tm, QK_DIM), lambda b, s: (b, 0, s, 0)),
            pl.BlockSpec((1, N_HEADS, tm, V_EXT), lambda b, s: (b, 0, s, 0)),
            pl.BlockSpec((1, tm, CONV_WIDTH), lambda b, s: (b, s, 0)),
        ],
        out_shape=[
            jax.ShapeDtypeStruct((B, N_HEADS, S, QK_DIM), BF16),
            jax.ShapeDtypeStruct((B, N_HEADS, S, QK_DIM), BF16),
            jax.ShapeDtypeStruct((B, N_HEADS, S, V_EXT), BF16),
            jax.ShapeDtypeStruct((B, S, CONV_WIDTH), BF16),
        ],
        scratch_shapes=[pltpu.VMEM((tm + 2 * SUBLANES, CONV_WIDTH), F32)],
        compiler_params=pltpu.CompilerParams(
            dimension_semantics=("arbitrary", "arbitrary"), vmem_limit_bytes=VMEM_LIMIT),
        name="inproj",
    )(x, positions, invf4, mix_norm, w_in_r, q_norm, w_uq_r, kv_norm, w_ukv, conv_w, conv_out_norm)


def _attn_kernel(q_ref, k_ref, v_ref, o_ref):
    tq = q_ref.shape[2]
    i = pl.program_id(1)
    neg = jnp.float32(-1e30)

    rq = tq // ATT_ROW_SPLIT
    chains = [(h, r) for h in range(N_HEADS) for r in range(ATT_ROW_SPLIT)]

    def step(j, carry, col_off=None):
        start = pl.multiple_of(j * TK, TK)
        new = []
        for c, (h, r) in enumerate(chains):
            m, acc = carry[c]
            if col_off is not None and col_off >= (r + 1) * rq:
                new.append((m, acc))
                continue
            kj = k_ref[0, h, pl.ds(start, TK), :]
            vj = v_ref[0, h, pl.ds(start, TK), :]
            s = lax.dot_general(q_ref[0, h, r * rq:(r + 1) * rq, :], kj, (((1,), (1,)), ((), ())),
                                preferred_element_type=F32)
            if col_off is not None and col_off + TK - 1 > r * rq:
                row = lax.broadcasted_iota(jnp.int32, (rq, TK), 0) + r * rq
                col = lax.broadcasted_iota(jnp.int32, (rq, TK), 1) + col_off
                s = jnp.where(col <= row, s, neg)
            m_new = jnp.maximum(m, jnp.max(s, axis=-1, keepdims=True))
            alpha = jnp.exp(m - m_new)
            p = jnp.exp(s - m_new)
            acc = alpha * acc + jnp.dot(p.astype(BF16), vj, preferred_element_type=F32)
            new.append((m_new, acc))
        return tuple(new)

    init = tuple((jnp.full((rq, 1), neg, F32), jnp.zeros((rq, V_EXT), F32)) for _ in chains)
    ratio = tq // TK
    final = lax.fori_loop(0, i * ratio, lambda j, c: step(j, c), init)
    for d in range(ratio):
        final = step(i * ratio + d, final, col_off=d * TK)
    for c, (h, r) in enumerate(chains):
        acc = final[c][1]
        o_ref[0, r * rq:(r + 1) * rq, h * V_HEAD_DIM:(h + 1) * V_HEAD_DIM] = (
            acc[:, :V_HEAD_DIM] / acc[:, V_HEAD_DIM:V_HEAD_DIM + 1]).astype(BF16)


def _attention(q, k, v):
    B, H, S, _ = q.shape
    assert TQ % TK == 0 and TQ % ATT_ROW_SPLIT == 0
    return pl.pallas_call(
        _attn_kernel,
        grid=(B, S // TQ),
        in_specs=[
            pl.BlockSpec((1, H, TQ, QK_DIM), lambda b, i: (b, 0, i, 0)),
            pl.BlockSpec((1, H, S, QK_DIM), lambda b, i: (b, 0, 0, 0)),
            pl.BlockSpec((1, H, S, V_EXT), lambda b, i: (b, 0, 0, 0)),
        ],
        out_specs=pl.BlockSpec((1, TQ, H * V_HEAD_DIM), lambda b, i: (b, i, 0)),
        out_shape=jax.ShapeDtypeStruct((B, S, H * V_HEAD_DIM), BF16),
        compiler_params=pltpu.CompilerParams(
            dimension_semantics=("arbitrary", "arbitrary"), vmem_limit_bytes=VMEM_LIMIT),
        name="attention",
    )(q, k, v)


def _outproj_kernel(x_ref, attn_ref, conv_ref, ga_ref, wo_ref, gmoe_ref, wr_hi_ref, wr_lo_ref, br_ref,
                    h1_ref, hn_ref, route_ref, gate_ref, counts_ref, carry_ref):
    tm = x_ref.shape[0]
    step = pl.program_id(0)

    @pl.when(step == 0)
    def _():
        carry_ref[...] = jnp.zeros_like(carry_ref)

    attn_n = _rms(attn_ref[...].astype(F32), ga_ref[...]).astype(BF16)
    mixed = jnp.concatenate([attn_n, conv_ref[...]], axis=1)
    h1 = x_ref[...] + jnp.dot(mixed, wo_ref[...], preferred_element_type=F32)
    h1_ref[...] = h1

    hn = _rms(h1, gmoe_ref[...])
    for j in range(ROW_CHUNKS):
        hn_ref[pl.ds(j, tm, stride=SUBLANES), :] = hn[:, j * LANES:(j + 1) * LANES]

    hn_hi = hn.astype(BF16)
    hn_lo = (hn - hn_hi.astype(F32)).astype(BF16)
    logits = (jnp.dot(hn_hi, wr_hi_ref[...], preferred_element_type=F32)
              + jnp.dot(hn_hi, wr_lo_ref[...], preferred_element_type=F32)
              + jnp.dot(hn_lo, wr_hi_ref[...], preferred_element_type=F32)) + br_ref[...]

    lane = lax.broadcasted_iota(jnp.int32, (tm, N_EXPERTS), 1).astype(F32)
    work = logits
    vals, idxs, hots = [], [], []
    for _ in range(TOP_K):
        mx = jnp.max(work, axis=-1, keepdims=True)
        idx = jnp.min(jnp.where(work == mx, lane, float(N_EXPERTS)), axis=-1, keepdims=True)
        hot = lane == idx
        work = jnp.where(hot, -jnp.inf, work)
        vals.append(mx)
        idxs.append(idx)
        hots.append(hot)

    exps = [jnp.exp(v - vals[0]) for v in vals]
    denom = exps[0] + exps[1] + exps[2] + exps[3]
    gates = [e / denom for e in exps]

    sel = (hots[0] | hots[1] | hots[2] | hots[3]).astype(F32)
    r = lax.broadcasted_iota(jnp.int32, (tm, tm), 0)
    c = lax.broadcasted_iota(jnp.int32, (tm, tm), 1)
    tri = (c < r).astype(BF16)
    rank_mat = jnp.dot(tri, sel.astype(BF16), preferred_element_type=F32) + carry_ref[...]
    carry_ref[...] = carry_ref[...] + jnp.sum(sel, axis=0, keepdims=True)
    counts_ref[...] = carry_ref[...].astype(jnp.int32)

    ranks = [jnp.sum(jnp.where(h, rank_mat, 0.0), axis=-1, keepdims=True) for h in hots]
    lane = lax.broadcasted_iota(jnp.int32, (tm, LANES), 1)
    route = jnp.zeros((tm, LANES), F32)
    gate_out = jnp.zeros((tm, TOP_K), F32)
    lane4 = lax.broadcasted_iota(jnp.int32, (tm, TOP_K), 1)
    for k in range(TOP_K):
        route = jnp.where(lane == k, idxs[k], route)
        route = jnp.where(lane == TOP_K + k, ranks[k], route)
        gate_out = jnp.where(lane4 == k, gates[k], gate_out)
    route_ref[...] = route.T[:2 * TOP_K, :].astype(jnp.int32)
    gate_ref[...] = gate_out


def _outproj(x2, attn2, conv2, attn_out_norm, w_o, moe_norm, w_router, b_router):
    N, D = x2.shape
    wr_hi = w_router.astype(BF16)
    wr_lo = (w_router - wr_hi.astype(F32)).astype(BF16)
    tm = TM_OUT
    const = lambda i: (0, 0)
    return pl.pallas_call(
        _outproj_kernel,
        grid=(N // tm,),
        in_specs=[
            pl.BlockSpec((tm, D), lambda i: (i, 0)),
            pl.BlockSpec((tm, ATTN_WIDTH), lambda i: (i, 0)),
            pl.BlockSpec((tm, CONV_WIDTH), lambda i: (i, 0)),
            pl.BlockSpec(attn_out_norm.shape, const),
            pl.BlockSpec(w_o.shape, const),
            pl.BlockSpec(moe_norm.shape, const),
            pl.BlockSpec(w_router.shape, const),
            pl.BlockSpec(w_router.shape, const),
            pl.BlockSpec(b_router.shape, const),
        ],
        out_specs=[
            pl.BlockSpec((tm, D), lambda i: (i, 0)),
            pl.BlockSpec((tm * SUBLANES, LANES), lambda i: (i, 0)),
            pl.BlockSpec((2 * TOP_K, tm), lambda i: (0, i)),
            pl.BlockSpec((tm, TOP_K), lambda i: (i, 0)),
            pl.BlockSpec((1, N_EXPERTS), const),
        ],
        out_shape=[
            jax.ShapeDtypeStruct((N, D), F32),
            jax.ShapeDtypeStruct((N * SUBLANES, LANES), F32),
            jax.ShapeDtypeStruct((2 * TOP_K, N), jnp.int32),
            jax.ShapeDtypeStruct((N, TOP_K), F32),
            jax.ShapeDtypeStruct((1, N_EXPERTS), jnp.int32),
        ],
        scratch_shapes=[pltpu.VMEM((1, N_EXPERTS), F32)],
        compiler_params=pltpu.CompilerParams(
            dimension_semantics=("arbitrary",), vmem_limit_bytes=VMEM_LIMIT),
        name="outproj_router",
    )(x2, attn2, conv2, attn_out_norm, w_o, moe_norm, wr_hi, wr_lo, b_router)


def _row_copy(src_ref, src_row, dst_ref, dst_row, sem):
    s = pl.multiple_of(src_row * SUBLANES, SUBLANES)
    d = pl.multiple_of(dst_row * SUBLANES, SUBLANES)
    return pltpu.make_async_copy(src_ref.at[pl.ds(s, SUBLANES), :], dst_ref.at[pl.ds(d, SUBLANES), :], sem)


def _dispatch_kernel(fill_lo_ref, fill_hi_ref, dest_ref, hn_ref, zero_ref, xs_ref, sems, zsem):
    step = pl.program_id(0)

    def zero_copy(first_row, n_rows):
        d = pl.multiple_of(first_row * SUBLANES, SUBLANES)
        return pltpu.make_async_copy(zero_ref.at[pl.ds(0, n_rows * SUBLANES), :],
                                     xs_ref.at[pl.ds(d, n_rows * SUBLANES), :], zsem)

    def fill_segments(start):
        def act(cp):
            if start:
                cp.start()
            else:
                cp.wait()

        def per_segment(e, _):
            lo = fill_lo_ref[e]
            n = fill_hi_ref[e] - lo
            n_full = n // ZERO_ROWS

            def full(c, _):
                act(zero_copy(lo + c * ZERO_ROWS, ZERO_ROWS))
                return 0
            lax.fori_loop(0, n_full, full, 0)
            off = lo + n_full * ZERO_ROWS
            rem = n - n_full * ZERO_ROWS
            size = ZERO_ROWS // 2
            while size >= 1:
                @pl.when((rem & size) != 0)
                def _(size=size):
                    act(zero_copy(off + (rem & ~(2 * size - 1)), size))
                size //= 2
            return 0
        lax.fori_loop(0, N_EXPERTS + 1, per_segment, 0)

    @pl.when(step == 0)
    def _():
        fill_segments(start=True)
        fill_segments(start=False)

    def batch(c, slot, start):
        def body(t, _):
            tok = c * DISPATCH_CHUNK + t
            for k in range(TOP_K):
                cp = _row_copy(hn_ref, tok, xs_ref, dest_ref[k, tok], sems.at[slot])
                if start:
                    cp.start(priority=k % 2)
                else:
                    cp.wait()
            return 0
        lax.fori_loop(0, DISPATCH_CHUNK, body, 0)

    def pipelined(cc, _):
        even = 2 * cc
        batch(even, 0, start=True)

        @pl.when(cc > 0)
        def _():
            batch(even - 1, 1, start=False)
        batch(even + 1, 1, start=True)
        batch(even, 0, start=False)
        return 0

    n_batches = TD // DISPATCH_CHUNK
    lax.fori_loop(0, n_batches // 2, pipelined, 0)
    batch(n_batches - 1, 1, start=False)


def _dispatch(fill_lo, fill_hi, dest_t, hn_rows, n_rows_padded):
    N = hn_rows.shape[0] // SUBLANES
    zero_row = jnp.zeros((ZERO_ROWS * SUBLANES, LANES), F32)
    grid_spec = pltpu.PrefetchScalarGridSpec(
        num_scalar_prefetch=2,
        grid=(N // TD,),
        in_specs=[
            pl.BlockSpec((TOP_K, TD), lambda i, *_: (0, i), memory_space=pltpu.SMEM),
            pl.BlockSpec((TD * SUBLANES, LANES), lambda i, *_: (i, 0)),
            pl.BlockSpec((ZERO_ROWS * SUBLANES, LANES), lambda i, *_: (0, 0)),
        ],
        out_specs=pl.BlockSpec(memory_space=pl.ANY),
        scratch_shapes=[pltpu.SemaphoreType.DMA((2,)), pltpu.SemaphoreType.DMA(())],
    )
    return pl.pallas_call(
        _dispatch_kernel,
        grid_spec=grid_spec,
        out_shape=jax.ShapeDtypeStruct((n_rows_padded * SUBLANES, LANES), F32),
        compiler_params=pltpu.CompilerParams(
            dimension_semantics=("arbitrary",), has_side_effects=True),
        name="dispatch",
    )(fill_lo, fill_hi, dest_t, hn_rows, zero_row)


def _ffn_kernel(blk_e_ref, nblk_ref, xs_ref, w1_ref, b1_ref, w2_ref, b2_ref, y_ref, w1b_ref, w2b_ref):
    i = pl.program_id(0)
    used = i < nblk_ref[0]

    @pl.when(used & ((i == 0) | (blk_e_ref[i] != blk_e_ref[jnp.maximum(i - 1, 0)])))
    def _():
        for r in range(0, D_MODEL, W_CAST_ROWS):
            w1b_ref[r:r + W_CAST_ROWS, :] = w1_ref[0, r:r + W_CAST_ROWS, :].astype(BF16)
        for r in range(0, D_EXPERT, W_CAST_ROWS):
            w2b_ref[r:r + W_CAST_ROWS, :] = w2_ref[0, r:r + W_CAST_ROWS, :].astype(BF16)

    @pl.when(used)
    def _():
        xb = jnp.concatenate(
            [xs_ref[pl.ds(j, TMF, stride=SUBLANES), :] for j in range(ROW_CHUNKS)], axis=1)
        h = jnp.dot(xb.astype(BF16), w1b_ref[...], preferred_element_type=F32) + b1_ref[0]
        gate = jnp.minimum(h[:, :D_EXPERT], SWIGLU_LIMIT)
        up = jnp.clip(h[:, D_EXPERT:], -SWIGLU_LIMIT, SWIGLU_LIMIT)
        act = gate * jax.nn.sigmoid(SWIGLU_ALPHA * gate) * (up + 1.0)
        y = jnp.dot(act.astype(BF16), w2b_ref[...], preferred_element_type=F32) + b2_ref[0]
        for j in range(ROW_CHUNKS):
            y_ref[pl.ds(j, TMF, stride=SUBLANES), :] = y[:, j * LANES:(j + 1) * LANES]

    @pl.when(i >= nblk_ref[0])
    def _():
        y_ref[...] = jnp.zeros_like(y_ref)


def _ffn(blk_expert, n_used, xs_rows, w1, b1, w2, b2):
    n_blocks = blk_expert.shape[0]

    def row_map(i, be, nb):
        return (i, 0)

    def w_map(i, be, nb):
        return (be[i], 0, 0)

    grid_spec = pltpu.PrefetchScalarGridSpec(
        num_scalar_prefetch=2,
        grid=(n_blocks,),
        in_specs=[
            pl.BlockSpec((TMF * SUBLANES, LANES), row_map),
            pl.BlockSpec((1, D_MODEL, 2 * D_EXPERT), w_map),
            pl.BlockSpec((1, 1, 2 * D_EXPERT), w_map),
            pl.BlockSpec((1, D_EXPERT, D_MODEL), w_map),
            pl.BlockSpec((1, 1, D_MODEL), w_map),
        ],
        out_specs=pl.BlockSpec((TMF * SUBLANES, LANES), row_map),
        scratch_shapes=[pltpu.VMEM((D_MODEL, 2 * D_EXPERT), BF16), pltpu.VMEM((D_EXPERT, D_MODEL), BF16)],
    )
    return pl.pallas_call(
        _ffn_kernel,
        grid_spec=grid_spec,
        out_shape=jax.ShapeDtypeStruct(xs_rows.shape, F32),
        compiler_params=pltpu.CompilerParams(
            dimension_semantics=("arbitrary",), vmem_limit_bytes=VMEM_LIMIT),
        name="expert_ffn",
    )(blk_expert, n_used, xs_rows, w1, b1, w2, b2)


def _final_kernel(dest_ref, dest_next_ref, h1_ref, gate_ref, p_ref, y_ref, gple_ref, wg_ref, wp_ref,
                  gpost_ref, gfin_ref, out_ref, ybuf0_ref, ybuf1_ref, sems):
    step = pl.program_id(0)
    n_steps = pl.num_programs(0)
    bufs = (ybuf0_ref, ybuf1_ref)

    def gather(idx_ref, slot, start):
        def body(t, _):
            for k in range(TOP_K):
                cp = _row_copy(y_ref, idx_ref[k, t], bufs[slot], k * TF + t, sems.at[slot])
                if start:
                    cp.start(priority=k % 2)
                else:
                    cp.wait()
            return 0
        lax.fori_loop(0, TF, body, 0)

    tokens_per_piece = TF // (ROW_CHUNKS * TOP_K)

    def start_piece(piece, slot):
        for t in range(piece * tokens_per_piece, (piece + 1) * tokens_per_piece):
            for k in range(TOP_K):
                _row_copy(y_ref, dest_next_ref[k, t], bufs[slot], k * TF + t,
                          sems.at[slot]).start(priority=k % 2)

    def compute(slot):
        g = gate_ref[...]
        gk = [jnp.broadcast_to(g[:, k:k + 1], (TF, LANES)) for k in range(TOP_K)]
        cols = []
        for j in range(ROW_CHUNKS):
            acc = gk[0] * bufs[slot][pl.ds(j, TF, stride=SUBLANES), :]
            start_piece(j * TOP_K, 1 - slot)
            for k in range(1, TOP_K):
                acc = acc + gk[k] * bufs[slot][pl.ds(k * TF * SUBLANES + j, TF, stride=SUBLANES), :]
                start_piece(j * TOP_K + k, 1 - slot)
            cols.append(acc)
        h2 = h1_ref[...] + jnp.concatenate(cols, axis=1)
        gate = jax.nn.sigmoid(jnp.dot(_rms(h2, gple_ref[...]).astype(BF16), wg_ref[...],
                                      preferred_element_type=F32))
        ple = _rms(jnp.dot(p_ref[...].astype(BF16), wp_ref[...], preferred_element_type=F32),
                   gpost_ref[...])
        out_ref[...] = _rms(h2 + gate * ple, gfin_ref[...])

    @pl.when(step == 0)
    def _():
        gather(dest_ref, 0, start=True)

    for slot in range(2):
        @pl.when(step % 2 == slot)
        def _(slot=slot):
            gather(dest_ref, slot, start=False)
            compute(slot)

            @pl.when(step + 1 == n_steps)
            def _():
                gather(dest_next_ref, 1 - slot, start=False)


def _final(dest_t, h1, gates, p2, y_rows, ple_norm, w_ple_gate, w_ple_proj, ple_post_norm, final_norm):
    N, D = h1.shape
    const = lambda i: (0, 0)
    n_steps = N // TF
    return pl.pallas_call(
        _final_kernel,
        grid=(n_steps,),
        in_specs=[
            pl.BlockSpec((TOP_K, TF), lambda i: (0, i), memory_space=pltpu.SMEM),
            pl.BlockSpec((TOP_K, TF), lambda i: (0, jnp.minimum(i + 1, n_steps - 1)),
                         memory_space=pltpu.SMEM),
            pl.BlockSpec((TF, D), lambda i: (i, 0)),
            pl.BlockSpec((TF, TOP_K), lambda i: (i, 0)),
            pl.BlockSpec((TF, PLE_DIM), lambda i: (i, 0)),
            pl.BlockSpec(memory_space=pl.ANY),
            pl.BlockSpec(ple_norm.shape, const),
            pl.BlockSpec(w_ple_gate.shape, const),
            pl.BlockSpec(w_ple_proj.shape, const),
            pl.BlockSpec(ple_post_norm.shape, const),
            pl.BlockSpec(final_norm.shape, const),
        ],
        out_specs=pl.BlockSpec((TF, D), lambda i: (i, 0)),
        out_shape=jax.ShapeDtypeStruct((N, D), F32),
        scratch_shapes=[pltpu.VMEM((TOP_K * TF * SUBLANES, LANES), F32),
                        pltpu.VMEM((TOP_K * TF * SUBLANES, LANES), F32),
                        pltpu.SemaphoreType.DMA((2,))],
        compiler_params=pltpu.CompilerParams(
            dimension_semantics=("arbitrary",), vmem_limit_bytes=VMEM_LIMIT),
        name="combine_ple_final",
    )(dest_t, dest_t, h1, gates, p2, y_rows, ple_norm, w_ple_gate, w_ple_proj, ple_post_norm,
      final_norm)


def _reorder_w_in(w):
    o = Q_LORA_RANK + KV_LORA_RANK
    return jnp.concatenate([w[:, :o], w[:, o + QK_ROPE_DIM:], w[:, o:o + QK_ROPE_DIM]], axis=1)


def _reorder_w_uq(w):
    w4 = w.reshape(Q_LORA_RANK, N_HEADS, QK_DIM)
    nope = w4[:, :, :QK_NOPE_DIM].reshape(Q_LORA_RANK, N_HEADS * QK_NOPE_DIM)
    r1 = w4[:, :, QK_NOPE_DIM:QK_NOPE_DIM + ROPE_HALF].reshape(Q_LORA_RANK, N_HEADS * ROPE_HALF)
    r2 = w4[:, :, QK_NOPE_DIM + ROPE_HALF:].reshape(Q_LORA_RANK, N_HEADS * ROPE_HALF)
    return jnp.concatenate([nope, r1, r2], axis=1)


def kernel(x, p, positions, mix_norm, w_in, q_norm, w_uq, kv_norm, w_ukv, conv_w, attn_out_norm, conv_out_norm, w_o, moe_norm, w_router, b_router, w1, b1, w2, b2, ple_norm, w_ple_gate, w_ple_proj, ple_post_norm, final_norm):
    B, S, D = x.shape
    N = B * S
    assert p.shape[0] == 1, "single layer"
    half = ROPE_HALF
    inv_freq = ROPE_THETA ** (-jnp.arange(half, dtype=F32) / half)
    invf_col = inv_freq[:, None]

    q, k, v, conv_n = _inproj(
        x, positions[:, None, :], invf_col, mix_norm, _reorder_w_in(w_in[0]).astype(BF16), q_norm,
        _reorder_w_uq(w_uq[0]).astype(BF16), kv_norm, w_ukv[0].astype(BF16), conv_w[0], conv_out_norm)
    attn = _attention(q, k, v)

    h1, hn_rows, route, gates, counts = _outproj(
        x.reshape(N, D), attn.reshape(N, ATTN_WIDTH), conv_n.reshape(N, CONV_WIDTH), attn_out_norm,
        w_o[0].astype(BF16), moe_norm, w_router[0], b_router)

    counts = counts[0]
    padded = ((counts + TMF - 1) // TMF) * TMF
    pad_ends = jnp.cumsum(padded)
    pad_starts = pad_ends - padded
    n_blocks = (N * TOP_K) // TMF + N_EXPERTS
    is_e = route[None, :TOP_K] == jnp.arange(N_EXPERTS, dtype=jnp.int32)[:, None, None]
    dest_t = (jnp.sum(jnp.where(is_e, pad_starts[:, None, None], 0), axis=0)
              + route[TOP_K:]).astype(jnp.int32)
    blk_start = jnp.arange(n_blocks, dtype=jnp.int32) * TMF
    blk_expert = jnp.minimum(
        jnp.sum((pad_ends[None, :] <= blk_start[:, None]).astype(jnp.int32), axis=1), N_EXPERTS - 1)
    n_used = (pad_ends[-1:] // TMF).astype(jnp.int32)

    total_rows = jnp.full((1,), n_blocks * TMF, jnp.int32)
    fill_lo = jnp.concatenate([pad_starts + counts, pad_ends[-1:]]).astype(jnp.int32)
    fill_hi = jnp.concatenate([pad_ends, total_rows]).astype(jnp.int32)
    xs_rows = _dispatch(fill_lo, fill_hi, dest_t, hn_rows, n_blocks * TMF)
    y_rows = _ffn(blk_expert, n_used, xs_rows, w1[0], b1[0][:, None, :], w2[0], b2[0][:, None, :])

    out = _final(dest_t, h1, gates, p[0].reshape(N, PLE_DIM), y_rows, ple_norm,
                 w_ple_gate[0].astype(BF16), w_ple_proj[0].astype(BF16), ple_post_norm,
                 final_norm.reshape(1, D))
    return out.reshape(B, S, D)
```

```python
import functools

import jax
import jax.numpy as jnp
from jax import lax
from jax.experimental import pallas as pl
from jax.experimental.pallas import tpu as pltpu

D_MODEL = 1024
PLE_DIM = 256
N_HEADS = 4
QK_NOPE_DIM = 128
QK_ROPE_DIM = 64
ROPE_HALF = QK_ROPE_DIM // 2
V_HEAD_DIM = 128
V_EXT = 2 * V_HEAD_DIM
QK_DIM = QK_NOPE_DIM + QK_ROPE_DIM
Q_LORA_RANK = 256
KV_LORA_RANK = 128
ROPE_THETA = 10000.0
ATTN_WIDTH = N_HEADS * V_HEAD_DIM
CONV_WIDTH = 512
CONV_K = 3
N_EXPERTS = 32
TOP_K = 4
D_EXPERT = 1024
SWIGLU_ALPHA = 1.702
SWIGLU_LIMIT = 7.0
EPS = 1e-6

SUBLANES = 8
LANES = 128
ROW_CHUNKS = D_MODEL // LANES

TM_IN = 512
TQ = 512
TK = 512
ATT_ROW_SPLIT = 2
TM_OUT = 512
TD = 512
DISPATCH_CHUNK = 64
ZERO_ROWS = 64
TMF = 512
FFN_SPLIT = 2
W_CAST_ROWS = 256
TF = 256

VMEM_LIMIT = 56 * 1024 * 1024

BF16 = jnp.bfloat16
F32 = jnp.float32


def _rms(x, g):
    return x * lax.rsqrt(jnp.mean(x * x, axis=-1, keepdims=True) + EPS) * g


def _inproj_kernel(x_ref, pos_ref, invf_ref, gmix_ref, win_ref, gq_ref, wuq_ref, gkv_ref, wukv_ref,
                   convw_ref, gconv_ref, q_ref, k_ref, v_ref, conv_ref, ubuf_ref):
    tm = x_ref.shape[1]
    s_idx = pl.program_id(1)

    xn = _rms(x_ref[0], gmix_ref[...])
    z = jnp.dot(xn.astype(BF16), win_ref[...], preferred_element_type=F32)
    c_q = z[:, 0:Q_LORA_RANK]
    c_kv = z[:, Q_LORA_RANK:Q_LORA_RANK + KV_LORA_RANK]
    o = Q_LORA_RANK + KV_LORA_RANK
    b_gate = z[:, o:o + CONV_WIDTH]
    c_gate = z[:, o + CONV_WIDTH:o + 2 * CONV_WIDTH]
    val = z[:, o + 2 * CONV_WIDTH:o + 3 * CONV_WIDTH]
    k_pe = z[:, o + 3 * CONV_WIDTH:o + 3 * CONV_WIDTH + QK_ROPE_DIM]

    ang_t = invf_ref[...] * pos_ref[0].astype(F32)
    cos_t = jnp.cos(ang_t)
    sin_t = jnp.sin(ang_t)
    cos4 = jnp.concatenate([cos_t] * N_HEADS, axis=0).T
    sin4 = jnp.concatenate([sin_t] * N_HEADS, axis=0).T

    q = jnp.dot(_rms(c_q, gq_ref[...]).astype(BF16), wuq_ref[...], preferred_element_type=F32)
    qn = N_HEADS * QK_NOPE_DIM
    x1 = q[:, qn:qn + LANES]
    x2 = q[:, qn + LANES:qn + 2 * LANES]
    o1 = x1 * cos4 - x2 * sin4
    o2 = x1 * sin4 + x2 * cos4
    scale = QK_DIM ** -0.5

    kv = jnp.dot(_rms(c_kv, gkv_ref[...]).astype(BF16), wukv_ref[...], preferred_element_type=F32)
    k1 = k_pe[:, :ROPE_HALF]
    k2 = k_pe[:, ROPE_HALF:]
    c1 = cos4[:, :ROPE_HALF]
    s1 = sin4[:, :ROPE_HALF]
    kr = jnp.concatenate([k1 * c1 - k2 * s1, k1 * s1 + k2 * c1], axis=1)

    for h in range(N_HEADS):
        qh = jnp.concatenate(
            [q[:, h * QK_NOPE_DIM:(h + 1) * QK_NOPE_DIM],
             o1[:, h * ROPE_HALF:(h + 1) * ROPE_HALF],
             o2[:, h * ROPE_HALF:(h + 1) * ROPE_HALF]], axis=1) * scale
        q_ref[0, h] = qh.astype(BF16)
        base = h * (QK_NOPE_DIM + V_HEAD_DIM)
        kh = jnp.concatenate([kv[:, base:base + QK_NOPE_DIM], kr], axis=1)
        k_ref[0, h] = kh.astype(BF16)
        vh = kv[:, base + QK_NOPE_DIM:base + QK_NOPE_DIM + V_HEAD_DIM]
        v_ref[0, h] = jnp.concatenate([vh, jnp.ones_like(vh)], axis=1).astype(BF16)

    u = c_gate * val

    @pl.when(s_idx == 0)
    def _():
        ubuf_ref[0:SUBLANES, :] = jnp.zeros((SUBLANES, CONV_WIDTH), F32)

    ubuf_ref[SUBLANES:SUBLANES + tm, :] = u
    u_m1 = ubuf_ref[SUBLANES - 1:SUBLANES - 1 + tm, :]
    u_m2 = ubuf_ref[SUBLANES - 2:SUBLANES - 2 + tm, :]
    cw = convw_ref[...]
    y = cw[0:1, :] * u_m2 + cw[1:2, :] * u_m1 + cw[2:3, :] * u
    conv_ref[0] = _rms(b_gate * y, gconv_ref[...]).astype(BF16)
    ubuf_ref[0:SUBLANES, :] = ubuf_ref[tm:tm + SUBLANES, :]


def _inproj(x, positions, invf4, mix_norm, w_in_r, q_norm, w_uq_r, kv_norm, w_ukv, conv_w, conv_out_norm):
    B, S, D = x.shape
    tm = TM_IN
    const = lambda b, s: (0, 0)
    return pl.pallas_call(
        _inproj_kernel,
        grid=(B, S // tm),
        in_specs=[
            pl.BlockSpec((1, tm, D), lambda b, s: (b, s, 0)),
            pl.BlockSpec((1, 1, tm), lambda b, s: (b, 0, s)),
            pl.BlockSpec(invf4.shape, const),
            pl.BlockSpec(mix_norm.shape, const),
            pl.BlockSpec(w_in_r.shape, const),
            pl.BlockSpec(q_norm.shape, const),
            pl.BlockSpec(w_uq_r.shape, const),
            pl.BlockSpec(kv_norm.shape, const),
            pl.BlockSpec(w_ukv.shape, const),
            pl.BlockSpec(conv_w.shape, const),
            pl.BlockSpec(conv_out_norm.shape, const),
        ],
        out_specs=[
            pl.BlockSpec((1, N_HEADS, tm, QK_DIM), lambda b, s: (b, 0, s, 0)),
            pl.BlockSpec((1, N_HEADS, tm, QK_DIM), lambda b, s: (b, 0, s, 0)),
            pl.BlockSpec((1, N_HEADS, tm, V_EXT), lambda b, s: (b, 0, s, 0)),
            pl.BlockSpec((1, tm, CONV_WIDTH), lambda b, s: (b, s, 0)),
        ],
        out_shape=[
            jax.ShapeDtypeStruct((B, N_HEADS, S, QK_DIM), BF16),
            jax.ShapeDtypeStruct((B, N_HEADS, S, QK_DIM), BF16),
            jax.ShapeDtypeStruct((B, N_HEADS, S, V_EXT), BF16),
            jax.ShapeDtypeStruct((B, S, CONV_WIDTH), BF16),
        ],
        scratch_shapes=[pltpu.VMEM((tm + 2 * SUBLANES, CONV_WIDTH), F32)],
        compiler_params=pltpu.CompilerParams(
            dimension_semantics=("arbitrary", "arbitrary"), vmem_limit_bytes=VMEM_LIMIT),
        name="inproj",
    )(x, positions, invf4, mix_norm, w_in_r, q_norm, w_uq_r, kv_norm, w_ukv, conv_w, conv_out_norm)


def _attn_kernel(q_ref, k_ref, v_ref, o_ref):
    tq = q_ref.shape[2]
    i = pl.program_id(1)
    neg = jnp.float32(-1e30)

    rq = tq // ATT_ROW_SPLIT
    chains = [(h, r) for h in range(N_HEADS) for r in range(ATT_ROW_SPLIT)]

    def step(j, carry, col_off=None):
        start = pl.multiple_of(j * TK, TK)
        new = []
        for c, (h, r) in enumerate(chains):
            m, acc = carry[c]
            if col_off is not None and col_off >= (r + 1) * rq:
                new.append((m, acc))
                continue
            kj = k_ref[0, h, pl.ds(start, TK), :]
            vj = v_ref[0, h, pl.ds(start, TK), :]
            s = lax.dot_general(q_ref[0, h, r * rq:(r + 1) * rq, :], kj, (((1,), (1,)), ((), ())),
                                preferred_element_type=F32)
            if col_off is not None and col_off + TK - 1 > r * rq:
                row = lax.broadcasted_iota(jnp.int32, (rq, TK), 0) + r * rq
                col = lax.broadcasted_iota(jnp.int32, (rq, TK), 1) + col_off
                s = jnp.where(col <= row, s, neg)
            m_new = jnp.maximum(m, jnp.max(s, axis=-1, keepdims=True))
            alpha = jnp.exp(m - m_new)
            p = jnp.exp(s - m_new)
            acc = alpha * acc + jnp.dot(p.astype(BF16), vj, preferred_element_type=F32)
            new.append((m_new, acc))
        return tuple(new)

    init = tuple((jnp.full((rq, 1), neg, F32), jnp.zeros((rq, V_EXT), F32)) for _ in chains)
    ratio = tq // TK
    final = lax.fori_loop(0, i * ratio, lambda j, c: step(j, c), init)
    for d in range(ratio):
        final = step(i * ratio + d, final, col_off=d * TK)
    for c, (h, r) in enumerate(chains):
        acc = final[c][1]
        o_ref[0, r * rq:(r + 1) * rq, h * V_HEAD_DIM:(h + 1) * V_HEAD_DIM] = (
            acc[:, :V_HEAD_DIM] / acc[:, V_HEAD_DIM:V_HEAD_DIM + 1]).astype(BF16)


def _attention(q, k, v):
    B, H, S, _ = q.shape
    assert TQ % TK == 0 and TQ % ATT_ROW_SPLIT == 0
    return pl.pallas_call(
        _attn_kernel,
        grid=(B, S // TQ),
        in_specs=[
            pl.BlockSpec((1, H, TQ, QK_DIM), lambda b, i: (b, 0, i, 0)),
            pl.BlockSpec((1, H, S, QK_DIM), lambda b, i: (b, 0, 0, 0)),
            pl.BlockSpec((1, H, S, V_EXT), lambda b, i: (b, 0, 0, 0)),
        ],
        out_specs=pl.BlockSpec((1, TQ, H * V_HEAD_DIM), lambda b, i: (b, i, 0)),
        out_shape=jax.ShapeDtypeStruct((B, S, H * V_HEAD_DIM), BF16),
        compiler_params=pltpu.CompilerParams(
            dimension_semantics=("arbitrary", "arbitrary"), vmem_limit_bytes=VMEM_LIMIT),
        name="attention",
    )(q, k, v)


def _outproj_kernel(x_ref, attn_ref, conv_ref, ga_ref, wo_ref, gmoe_ref, wr_hi_ref, wr_lo_ref, br_ref,
                    h1_ref, hn_ref, route_ref, gate_ref, counts_ref, carry_ref):
    tm = x_ref.shape[0]
    step = pl.program_id(0)

    @pl.when(step == 0)
    def _():
        carry_ref[...] = jnp.zeros_like(carry_ref)

    attn_n = _rms(attn_ref[...].astype(F32), ga_ref[...]).astype(BF16)
    mixed = jnp.concatenate([attn_n, conv_ref[...]], axis=1)
    h1 = x_ref[...] + jnp.dot(mixed, wo_ref[...], preferred_element_type=F32)
    h1_ref[...] = h1

    hn = _rms(h1, gmoe_ref[...])
    for j in range(ROW_CHUNKS):
        hn_ref[pl.ds(j, tm, stride=SUBLANES), :] = hn[:, j * LANES:(j + 1) * LANES]

    hn_hi = hn.astype(BF16)
    hn_lo = (hn - hn_hi.astype(F32)).astype(BF16)
    logits = (jnp.dot(hn_hi, wr_hi_ref[...], preferred_element_type=F32)
              + jnp.dot(hn_hi, wr_lo_ref[...], preferred_element_type=F32)
              + jnp.dot(hn_lo, wr_hi_ref[...], preferred_element_type=F32)) + br_ref[...]

    lane = lax.broadcasted_iota(jnp.int32, (tm, N_EXPERTS), 1).astype(F32)
    work = logits
    vals, idxs, hots = [], [], []
    for _ in range(TOP_K):
        mx = jnp.max(work, axis=-1, keepdims=True)
        idx = jnp.min(jnp.where(work == mx, lane, float(N_EXPERTS)), axis=-1, keepdims=True)
        hot = lane == idx
        work = jnp.where(hot, -jnp.inf, work)
        vals.append(mx)
        idxs.append(idx)
        hots.append(hot)

    exps = [jnp.exp(v - vals[0]) for v in vals]
    denom = exps[0] + exps[1] + exps[2] + exps[3]
    gates = [e / denom for e in exps]

    sel = (hots[0] | hots[1] | hots[2] | hots[3]).astype(F32)
    r = lax.broadcasted_iota(jnp.int32, (tm, tm), 0)
    c = lax.broadcasted_iota(jnp.int32, (tm, tm), 1)
    tri = (c < r).astype(BF16)
    rank_mat = jnp.dot(tri, sel.astype(BF16), preferred_element_type=F32) + carry_ref[...]
    carry_ref[...] = carry_ref[...] + jnp.sum(sel, axis=0, keepdims=True)
    counts_ref[...] = carry_ref[...].astype(jnp.int32)

    ranks = [jnp.sum(jnp.where(h, rank_mat, 0.0), axis=-1, keepdims=True) for h in hots]
    lane = lax.broadcasted_iota(jnp.int32, (tm, LANES), 1)
    route = jnp.zeros((tm, LANES), F32)
    gate_out = jnp.zeros((tm, TOP_K), F32)
    lane4 = lax.broadcasted_iota(jnp.int32, (tm, TOP_K), 1)
    for k in range(TOP_K):
        route = jnp.where(lane == k, idxs[k], route)
        route = jnp.where(lane == TOP_K + k, ranks[k], route)
        gate_out = jnp.where(lane4 == k, gates[k], gate_out)
    route_ref[...] = route.T[:2 * TOP_K, :].astype(jnp.int32)
    gate_ref[...] = gate_out


def _outproj(x2, attn2, conv2, attn_out_norm, w_o, moe_norm, w_router, b_router):
    N, D = x2.shape
    wr_hi = w_router.astype(BF16)
    wr_lo = (w_router - wr_hi.astype(F32)).astype(BF16)
    tm = TM_OUT
    const = lambda i: (0, 0)
    return pl.pallas_call(
        _outproj_kernel,
        grid=(N // tm,),
        in_specs=[
            pl.BlockSpec((tm, D), lambda i: (i, 0)),
            pl.BlockSpec((tm, ATTN_WIDTH), lambda i: (i, 0)),
            pl.BlockSpec((tm, CONV_WIDTH), lambda i: (i, 0)),
            pl.BlockSpec(attn_out_norm.shape, const),
            pl.BlockSpec(w_o.shape, const),
            pl.BlockSpec(moe_norm.shape, const),
            pl.BlockSpec(w_router.shape, const),
            pl.BlockSpec(w_router.shape, const),
            pl.BlockSpec(b_router.shape, const),
        ],
        out_specs=[
            pl.BlockSpec((tm, D), lambda i: (i, 0)),
            pl.BlockSpec((tm * SUBLANES, LANES), lambda i: (i, 0)),
            pl.BlockSpec((2 * TOP_K, tm), lambda i: (0, i)),
            pl.BlockSpec((tm, TOP_K), lambda i: (i, 0)),
            pl.BlockSpec((1, N_EXPERTS), const),
        ],
        out_shape=[
            jax.ShapeDtypeStruct((N, D), F32),
            jax.ShapeDtypeStruct((N * SUBLANES, LANES), F32),
            jax.ShapeDtypeStruct((2 * TOP_K, N), jnp.int32),
            jax.ShapeDtypeStruct((N, TOP_K), F32),
            jax.ShapeDtypeStruct((1, N_EXPERTS), jnp.int32),
        ],
        scratch_shapes=[pltpu.VMEM((1, N_EXPERTS), F32)],
        compiler_params=pltpu.CompilerParams(
            dimension_semantics=("arbitrary",), vmem_limit_bytes=VMEM_LIMIT),
        name="outproj_router",
    )(x2, attn2, conv2, attn_out_norm, w_o, moe_norm, wr_hi, wr_lo, b_router)


def _row_copy(src_ref, src_row, dst_ref, dst_row, sem):
    s = pl.multiple_of(src_row * SUBLANES, SUBLANES)
    d = pl.multiple_of(dst_row * SUBLANES, SUBLANES)
    return pltpu.make_async_copy(src_ref.at[pl.ds(s, SUBLANES), :], dst_ref.at[pl.ds(d, SUBLANES), :], sem)


def _dispatch_kernel(fill_lo_ref, fill_hi_ref, dest_ref, hn_ref, zero_ref, xs_ref, sems, zsem):
    step = pl.program_id(0)

    def zero_copy(first_row, n_rows):
        d = pl.multiple_of(first_row * SUBLANES, SUBLANES)
        return pltpu.make_async_copy(zero_ref.at[pl.ds(0, n_rows * SUBLANES), :],
                                     xs_ref.at[pl.ds(d, n_rows * SUBLANES), :], zsem)

    def fill_segments(start):
        def act(cp):
            if start:
                cp.start()
            else:
                cp.wait()

        def per_segment(e, _):
            lo = fill_lo_ref[e]
            n = fill_hi_ref[e] - lo
            n_full = n // ZERO_ROWS

            def full(c, _):
                act(zero_copy(lo + c * ZERO_ROWS, ZERO_ROWS))
                return 0
            lax.fori_loop(0, n_full, full, 0)
            off = lo + n_full * ZERO_ROWS
            rem = n - n_full * ZERO_ROWS
            size = ZERO_ROWS // 2
            while size >= 1:
                @pl.when((rem & size) != 0)
                def _(size=size):
                    act(zero_copy(off + (rem & ~(2 * size - 1)), size))
                size //= 2
            return 0
        lax.fori_loop(0, N_EXPERTS + 1, per_segment, 0)

    @pl.when(step == 0)
    def _():
        fill_segments(start=True)
        fill_segments(start=False)

    def batch(c, slot, start):
        def body(t, _):
            tok = c * DISPATCH_CHUNK + t
            for k in range(TOP_K):
                cp = _row_copy(hn_ref, tok, xs_ref, dest_ref[k, tok], sems.at[slot])
                if start:
                    cp.start(priority=k % 2)
                else:
                    cp.wait()
            return 0
        lax.fori_loop(0, DISPATCH_CHUNK, body, 0)

    def pipelined(cc, _):
        even = 2 * cc
        batch(even, 0, start=True)

        @pl.when(cc > 0)
        def _():
            batch(even - 1, 1, start=False)
        batch(even + 1, 1, start=True)
        batch(even, 0, start=False)
        return 0

    n_batches = TD // DISPATCH_CHUNK
    lax.fori_loop(0, n_batches // 2, pipelined, 0)
    batch(n_batches - 1, 1, start=False)


def _dispatch(fill_lo, fill_hi, dest_t, hn_rows, n_rows_padded):
    N = hn_rows.shape[0] // SUBLANES
    zero_row = jnp.zeros((ZERO_ROWS * SUBLANES, LANES), F32)
    grid_spec = pltpu.PrefetchScalarGridSpec(
        num_scalar_prefetch=2,
        grid=(N // TD,),
        in_specs=[
            pl.BlockSpec((TOP_K, TD), lambda i, *_: (0, i), memory_space=pltpu.SMEM),
            pl.BlockSpec((TD * SUBLANES, LANES), lambda i, *_: (i, 0)),
            pl.BlockSpec((ZERO_ROWS * SUBLANES, LANES), lambda i, *_: (0, 0)),
        ],
        out_specs=pl.BlockSpec(memory_space=pl.ANY),
        scratch_shapes=[pltpu.SemaphoreType.DMA((2,)), pltpu.SemaphoreType.DMA(())],
    )
    return pl.pallas_call(
        _dispatch_kernel,
        grid_spec=grid_spec,
        out_shape=jax.ShapeDtypeStruct((n_rows_padded * SUBLANES, LANES), F32),
        compiler_params=pltpu.CompilerParams(
            dimension_semantics=("arbitrary",), has_side_effects=True),
        name="dispatch",
    )(fill_lo, fill_hi, dest_t, hn_rows, zero_row)


def _ffn_kernel(blk_e_ref, nblk_ref, xs_ref, w1_ref, b1_ref, w2_ref, b2_ref, y_ref, w1b_ref, w2b_ref):
    i = pl.program_id(0)
    used = i < nblk_ref[0]

    @pl.when(used & ((i == 0) | (blk_e_ref[i] != blk_e_ref[jnp.maximum(i - 1, 0)])))
    def _():
        for r in range(0, D_MODEL, W_CAST_ROWS):
            w1b_ref[r:r + W_CAST_ROWS, :] = w1_ref[0, r:r + W_CAST_ROWS, :].astype(BF16)
        for r in range(0, D_EXPERT, W_CAST_ROWS):
            w2b_ref[r:r + W_CAST_ROWS, :] = w2_ref[0, r:r + W_CAST_ROWS, :].astype(BF16)

    @pl.when(used)
    def _():
        xb = jnp.concatenate(
            [xs_ref[pl.ds(j, TMF, stride=SUBLANES), :] for j in range(ROW_CHUNKS)], axis=1)
        xb = xb.astype(BF16)
        b1 = b1_ref[0]
        y = b2_ref[0]
        hw = D_EXPERT // FFN_SPLIT
        for c in range(FFN_SPLIT):
            g0, u0 = c * hw, D_EXPERT + c * hw
            gate = jnp.dot(xb, w1b_ref[:, g0:g0 + hw], preferred_element_type=F32) + b1[:, g0:g0 + hw]
            up = jnp.dot(xb, w1b_ref[:, u0:u0 + hw], preferred_element_type=F32) + b1[:, u0:u0 + hw]
            gate = jnp.minimum(gate, SWIGLU_LIMIT)
            up = jnp.clip(up, -SWIGLU_LIMIT, SWIGLU_LIMIT)
            act = gate * jax.nn.sigmoid(SWIGLU_ALPHA * gate) * (up + 1.0)
            y = y + jnp.dot(act.astype(BF16), w2b_ref[g0:g0 + hw, :], preferred_element_type=F32)
        for j in range(ROW_CHUNKS):
            y_ref[pl.ds(j, TMF, stride=SUBLANES), :] = y[:, j * LANES:(j + 1) * LANES]

    @pl.when(i >= nblk_ref[0])
    def _():
        y_ref[...] = jnp.zeros_like(y_ref)


def _ffn(blk_expert, n_used, xs_rows, w1, b1, w2, b2):
    n_blocks = blk_expert.shape[0]

    def row_map(i, be, nb):
        return (i, 0)

    def w_map(i, be, nb):
        return (be[i], 0, 0)

    grid_spec = pltpu.PrefetchScalarGridSpec(
        num_scalar_prefetch=2,
        grid=(n_blocks,),
        in_specs=[
            pl.BlockSpec((TMF * SUBLANES, LANES), row_map),
            pl.BlockSpec((1, D_MODEL, 2 * D_EXPERT), w_map),
            pl.BlockSpec((1, 1, 2 * D_EXPERT), w_map),
            pl.BlockSpec((1, D_EXPERT, D_MODEL), w_map),
            pl.BlockSpec((1, 1, D_MODEL), w_map),
        ],
        out_specs=pl.BlockSpec((TMF * SUBLANES, LANES), row_map),
        scratch_shapes=[pltpu.VMEM((D_MODEL, 2 * D_EXPERT), BF16), pltpu.VMEM((D_EXPERT, D_MODEL), BF16)],
    )
    return pl.pallas_call(
        _ffn_kernel,
        grid_spec=grid_spec,
        out_shape=jax.ShapeDtypeStruct(xs_rows.shape, F32),
        compiler_params=pltpu.CompilerParams(
            dimension_semantics=("arbitrary",), vmem_limit_bytes=VMEM_LIMIT),
        name="expert_ffn",
    )(blk_expert, n_used, xs_rows, w1, b1, w2, b2)


def _final_kernel(dest_ref, dest_next_ref, h1_ref, gate_ref, p_ref, y_ref, gple_ref, wg_ref, wp_ref,
                  gpost_ref, gfin_ref, out_ref, ybuf0_ref, ybuf1_ref, sems):
    step = pl.program_id(0)
    n_steps = pl.num_programs(0)
    bufs = (ybuf0_ref, ybuf1_ref)

    def gather(idx_ref, slot, start):
        def body(t, _):
            for k in range(TOP_K):
                cp = _row_copy(y_ref, idx_ref[k, t], bufs[slot], k * TF + t, sems.at[slot])
                if start:
                    cp.start(priority=k % 2)
                else:
                    cp.wait()
            return 0
        lax.fori_loop(0, TF, body, 0)

    tokens_per_piece = TF // (ROW_CHUNKS * TOP_K)

    def start_piece(piece, slot):
        for t in range(piece * tokens_per_piece, (piece + 1) * tokens_per_piece):
            for k in range(TOP_K):
                _row_copy(y_ref, dest_next_ref[k, t], bufs[slot], k * TF + t,
                          sems.at[slot]).start(priority=k % 2)

    def compute(slot):
        g = gate_ref[...]
        gk = [jnp.broadcast_to(g[:, k:k + 1], (TF, LANES)) for k in range(TOP_K)]
        cols = []
        for j in range(ROW_CHUNKS):
            acc = gk[0] * bufs[slot][pl.ds(j, TF, stride=SUBLANES), :]
            start_piece(j * TOP_K, 1 - slot)
            for k in range(1, TOP_K):
                acc = acc + gk[k] * bufs[slot][pl.ds(k * TF * SUBLANES + j, TF, stride=SUBLANES), :]
                start_piece(j * TOP_K + k, 1 - slot)
            cols.append(acc)
        h2 = h1_ref[...] + jnp.concatenate(cols, axis=1)
        gate = jax.nn.sigmoid(jnp.dot(_rms(h2, gple_ref[...]).astype(BF16), wg_ref[...],
                                      preferred_element_type=F32))
        ple = _rms(jnp.dot(p_ref[...].astype(BF16), wp_ref[...], preferred_element_type=F32),
                   gpost_ref[...])
        out_ref[...] = _rms(h2 + gate * ple, gfin_ref[...])

    @pl.when(step == 0)
    def _():
        gather(dest_ref, 0, start=True)

    for slot in range(2):
        @pl.when(step % 2 == slot)
        def _(slot=slot):
            gather(dest_ref, slot, start=False)
            compute(slot)

            @pl.when(step + 1 == n_steps)
            def _():
                gather(dest_next_ref, 1 - slot, start=False)


def _final(dest_t, h1, gates, p2, y_rows, ple_norm, w_ple_gate, w_ple_proj, ple_post_norm, final_norm):
    N, D = h1.shape
    const = lambda i: (0, 0)
    n_steps = N // TF
    return pl.pallas_call(
        _final_kernel,
        grid=(n_steps,),
        in_specs=[
            pl.BlockSpec((TOP_K, TF), lambda i: (0, i), memory_space=pltpu.SMEM),
            pl.BlockSpec((TOP_K, TF), lambda i: (0, jnp.minimum(i + 1, n_steps - 1)),
                         memory_space=pltpu.SMEM),
            pl.BlockSpec((TF, D), lambda i: (i, 0)),
            pl.BlockSpec((TF, TOP_K), lambda i: (i, 0)),
            pl.BlockSpec((TF, PLE_DIM), lambda i: (i, 0)),
            pl.BlockSpec(memory_space=pl.ANY),
            pl.BlockSpec(ple_norm.shape, const),
            pl.BlockSpec(w_ple_gate.shape, const),
            pl.BlockSpec(w_ple_proj.shape, const),
            pl.BlockSpec(ple_post_norm.shape, const),
            pl.BlockSpec(final_norm.shape, const),
        ],
        out_specs=pl.BlockSpec((TF, D), lambda i: (i, 0)),
        out_shape=jax.ShapeDtypeStruct((N, D), F32),
        scratch_shapes=[pltpu.VMEM((TOP_K * TF * SUBLANES, LANES), F32),
                        pltpu.VMEM((TOP_K * TF * SUBLANES, LANES), F32),
                        pltpu.SemaphoreType.DMA((2,))],
        compiler_params=pltpu.CompilerParams(
            dimension_semantics=("arbitrary",), vmem_limit_bytes=VMEM_LIMIT),
        name="combine_ple_final",
    )(dest_t, dest_t, h1, gates, p2, y_rows, ple_norm, w_ple_gate, w_ple_proj, ple_post_norm,
      final_norm)


def _reorder_w_in(w):
    o = Q_LORA_RANK + KV_LORA_RANK
    return jnp.concatenate([w[:, :o], w[:, o + QK_ROPE_DIM:], w[:, o:o + QK_ROPE_DIM]], axis=1)


def _reorder_w_uq(w):
    w4 = w.reshape(Q_LORA_RANK, N_HEADS, QK_DIM)
    nope = w4[:, :, :QK_NOPE_DIM].reshape(Q_LORA_RANK, N_HEADS * QK_NOPE_DIM)
    r1 = w4[:, :, QK_NOPE_DIM:QK_NOPE_DIM + ROPE_HALF].reshape(Q_LORA_RANK, N_HEADS * ROPE_HALF)
    r2 = w4[:, :, QK_NOPE_DIM + ROPE_HALF:].reshape(Q_LORA_RANK, N_HEADS * ROPE_HALF)
    return jnp.concatenate([nope, r1, r2], axis=1)


def kernel(x, p, positions, mix_norm, w_in, q_norm, w_uq, kv_norm, w_ukv, conv_w, attn_out_norm, conv_out_norm, w_o, moe_norm, w_router, b_router, w1, b1, w2, b2, ple_norm, w_ple_gate, w_ple_proj, ple_post_norm, final_norm):
    B, S, D = x.shape
    N = B * S
    assert p.shape[0] == 1, "single layer"
    half = ROPE_HALF
    inv_freq = ROPE_THETA ** (-jnp.arange(half, dtype=F32) / half)
    invf_col = inv_freq[:, None]

    q, k, v, conv_n = _inproj(
        x, positions[:, None, :], invf_col, mix_norm, _reorder_w_in(w_in[0]).astype(BF16), q_norm,
        _reorder_w_uq(w_uq[0]).astype(BF16), kv_norm, w_ukv[0].astype(BF16), conv_w[0], conv_out_norm)
    attn = _attention(q, k, v)

    h1, hn_rows, route, gates, counts = _outproj(
        x.reshape(N, D), attn.reshape(N, ATTN_WIDTH), conv_n.reshape(N, CONV_WIDTH), attn_out_norm,
        w_o[0].astype(BF16), moe_norm, w_router[0], b_router)

    counts = counts[0]
    padded = ((counts + TMF - 1) // TMF) * TMF
    pad_ends = jnp.cumsum(padded)
    pad_starts = pad_ends - padded
    n_blocks = (N * TOP_K) // TMF + N_EXPERTS
    is_e = route[None, :TOP_K] == jnp.arange(N_EXPERTS, dtype=jnp.int32)[:, None, None]
    dest_t = (jnp.sum(jnp.where(is_e, pad_starts[:, None, None], 0), axis=0)
              + route[TOP_K:]).astype(jnp.int32)
    blk_start = jnp.arange(n_blocks, dtype=jnp.int32) * TMF
    blk_expert = jnp.minimum(
        jnp.sum((pad_ends[None, :] <= blk_start[:, None]).astype(jnp.int32), axis=1), N_EXPERTS - 1)
    n_used = (pad_ends[-1:] // TMF).astype(jnp.int32)

    total_rows = jnp.full((1,), n_blocks * TMF, jnp.int32)
    fill_lo = jnp.concatenate([pad_starts + counts, pad_ends[-1:]]).astype(jnp.int32)
    fill_hi = jnp.concatenate([pad_ends, total_rows]).astype(jnp.int32)
    xs_rows = _dispatch(fill_lo, fill_hi, dest_t, hn_rows, n_blocks * TMF)
    y_rows = _ffn(blk_expert, n_used, xs_rows, w1[0], b1[0][:, None, :], w2[0], b2[0][:, None, :])

    out = _final(dest_t, h1, gates, p[0].reshape(N, PLE_DIM), y_rows, ple_norm,
                 w_ple_gate[0].astype(BF16), w_ple_proj[0].astype(BF16), ple_post_norm,
                 final_norm.reshape(1, D))
    return out.reshape(B, S, D)
```
